```python
import math
import jax, jax.numpy as jnp
from jax import lax
import numpy as np

D_MODEL = 2048
BATCH = 2
SEQ = 8192
DEPTH = 1

CHUNK = 64
RET_WIDTH = D_MODEL // 2
RET_HEADS = 8
RET_HEAD_DIM = RET_WIDTH // RET_HEADS
SSM_WIDTH = D_MODEL - RET_WIDTH
SSM_GROUP = 16
SSM_GROUPS = SSM_WIDTH // SSM_GROUP
SSM_STATE = 64
D_FF = -(-8 * D_MODEL // (3 * 256)) * 256
IN_WIDTH = 4 * RET_WIDTH + SSM_WIDTH
ROPE_BASE = 10000.0
EPS = 1e-6

kernel_name = "hybrid_retention_s5_block"


def rmsnorm(x, g):
    xf = x.astype(jnp.float32)
    y = xf * lax.rsqrt(jnp.mean(xf * xf, axis=-1, keepdims=True) + EPS) * g.astype(jnp.float32)
    return y.astype(x.dtype)


def rope(x, pos):
    dh = x.shape[-1]
    half = dh // 2
    freqs = ROPE_BASE ** (-jnp.arange(half, dtype=jnp.float32) / half)
    ang = pos[:, None] * freqs[None, :]
    cos = jnp.cos(ang)[None, :, None, :]
    sin = jnp.sin(ang)[None, :, None, :]
    xf = x.astype(jnp.float32)
    x1, x2 = xf[..., :half], xf[..., half:]
    return jnp.concatenate([x1 * cos - x2 * sin, x2 * cos + x1 * sin], axis=-1)


def retention_group(q, k, v, g, gn_gain):
    b, s, h, dk = q.shape
    n_chunks = s // CHUNK
    hh = jnp.arange(h, dtype=jnp.float32)
    log_g = jnp.log1p(-(2.0 ** (-5.0 - hh)))
    q = q.reshape(b, n_chunks, CHUNK, h, dk)
    k = k.reshape(b, n_chunks, CHUNK, h, dk) * (dk ** -0.5)
    v = v.astype(jnp.float32).reshape(b, n_chunks, CHUNK, h, dk)
    idx = jnp.arange(CHUNK, dtype=jnp.float32)
    dist = jnp.abs(idx[:, None] - idx[None, :])
    intra_decay = jnp.exp(log_g[:, None, None] * dist)
    scores = jnp.einsum('bnqhd,bnkhd->bnhqk', q, k) * intra_decay
    intra = jnp.einsum('bnhqk,bnkhe->bnqhe', scores, v)
    k_dec = jnp.exp(log_g[None, :] * (CHUNK - 1 - idx)[:, None])
    kv = jnp.einsum('bnchd,bnche->nbhde', k * k_dec[..., None], v)
    chunk_dec = jnp.exp(log_g * CHUNK)[:, None, None]

    def step(state, kv_i):
        return state * chunk_dec + kv_i, state

    _, state_prev = lax.scan(step, jnp.zeros_like(kv[0]), kv)
    q_dec = jnp.exp(log_g[None, :] * (idx + 1.0)[:, None])
    inter = jnp.einsum('bnchd,nbhde->bnche', q * q_dec[..., None], state_prev)
    out = (intra + inter).reshape(b, s, h, dk)
    mu = jnp.mean(out, axis=-1, keepdims=True)
    var = jnp.mean(jnp.square(out - mu), axis=-1, keepdims=True)
    out = ((out - mu) * lax.rsqrt(var + EPS)).reshape(b, s, h * dk) * gn_gain.astype(jnp.float32)
    return jax.nn.silu(g.astype(jnp.float32)) * out


def s5_group(u, a_re, a_im, log_dt, b_re, b_im, c_re, c_im, d_skip, w_glu, b_glu, out_gain):
    b, s, _ = u.shape
    f32 = jnp.float32
    uf = u.astype(f32).reshape(b, s, SSM_GROUPS, SSM_GROUP)
    lam = lax.complex(a_re.astype(f32), a_im.astype(f32))
    dt = jnp.exp(log_dt.astype(f32))[:, None]
    lam_bar = jnp.exp(lam * dt)
    b_c = lax.complex(b_re.astype(f32), b_im.astype(f32))
    b_bar = ((lam_bar - 1.0) / lam)[..., None] * b_c
    bu = lax.complex(jnp.einsum('gpc,bsgc->bsgp', jnp.real(b_bar), uf),
                     jnp.einsum('gpc,bsgc->bsgp', jnp.imag(b_bar), uf))
    a = jnp.broadcast_to(lam_bar, bu.shape)

    def combine(left, right):
        a_l, b_l = left
        a_r, b_r = right
        return a_r * a_l, a_r * b_l + b_r

    _, states = lax.associative_scan(combine, (a, bu), axis=1)
    y = (jnp.einsum('gcp,bsgp->bsgc', c_re.astype(f32), jnp.real(states))
         - jnp.einsum('gcp,bsgp->bsgc', c_im.astype(f32), jnp.imag(states)))
    y = (y + d_skip.astype(f32).reshape(SSM_GROUPS, SSM_GROUP) * uf).reshape(b, s, SSM_WIDTH)
    y1 = jax.nn.gelu(y)
    y = y1 * jax.nn.sigmoid(y1 @ w_glu.astype(f32) + b_glu.astype(f32))
    return rmsnorm(y, out_gain)


def setup_inputs(seed: int = 0) -> dict:
    key = jax.random.key(seed)
    ks = jax.random.split(key, 24)
    f32 = jnp.float32
    L = DEPTH

    def nrm(k, shape, scale):
        return jax.random.normal(k, shape, f32) * scale

    def gain(k, shape):
        return 1.0 + 0.02 * jax.random.normal(k, shape, f32)

    n = jnp.arange(SSM_STATE, dtype=f32)
    return {
        "x": nrm(ks[0], (BATCH, SEQ, D_MODEL), 1.0),
        "norm_mix_g": gain(ks[1], (L, D_MODEL)),
        "w_in": nrm(ks[2], (L, D_MODEL, IN_WIDTH), D_MODEL ** -0.5),
        "ret_gn_g": gain(ks[3], (L, RET_WIDTH)),
        "ssm_a_re": -0.5 + 0.01 * nrm(ks[4], (L, SSM_GROUPS, SSM_STATE), 1.0),
        "ssm_a_im": math.pi * n + 0.01 * nrm(ks[5], (L, SSM_GROUPS, SSM_STATE), 1.0),
        "ssm_log_dt": jax.random.uniform(ks[6], (L, SSM_GROUPS), f32, math.log(1e-3), math.log(1e-1)),
        "ssm_b_re": nrm(ks[7], (L, SSM_GROUPS, SSM_STATE, SSM_GROUP), (2 * SSM_GROUP) ** -0.5),
        "ssm_b_im": nrm(ks[8], (L, SSM_GROUPS, SSM_STATE, SSM_GROUP), (2 * SSM_GROUP) ** -0.5),
        "ssm_c_re": nrm(ks[9], (L, SSM_GROUPS, SSM_GROUP, SSM_STATE), (2 * SSM_STATE) ** -0.5),
        "ssm_c_im": nrm(ks[10], (L, SSM_GROUPS, SSM_GROUP, SSM_STATE), (2 * SSM_STATE) ** -0.5),
        "ssm_d": nrm(ks[11], (L, SSM_WIDTH), 1.0),
        "ssm_w_glu": nrm(ks[12], (L, SSM_WIDTH, SSM_WIDTH), SSM_WIDTH ** -0.5),
        "ssm_b_glu": nrm(ks[13], (L, SSM_WIDTH), 0.01),
        "ssm_out_g": gain(ks[14], (L, SSM_WIDTH)),
        "w_out": nrm(ks[15], (L, D_MODEL, D_MODEL), D_MODEL ** -0.5),
        "norm_ffn_g": gain(ks[16], (L, D_MODEL)),
        "w_gate": nrm(ks[17], (L, D_MODEL, D_FF), D_MODEL ** -0.5),
        "w_up": nrm(ks[18], (L, D_MODEL, D_FF), D_MODEL ** -0.5),
        "w_down": nrm(ks[19], (L, D_FF, D_MODEL), D_FF ** -0.5),
        "norm_final_g": gain(ks[20], (D_MODEL,)),
    }


def reference(x, norm_mix_g, w_in, ret_gn_g, ssm_a_re, ssm_a_im, ssm_log_dt, ssm_b_re, ssm_b_im,
              ssm_c_re, ssm_c_im, ssm_d, ssm_w_glu, ssm_b_glu, ssm_out_g, w_out, norm_ffn_g,
              w_gate, w_up, w_down, norm_final_g):
    b, s, _ = x.shape
    pos = jnp.arange(s, dtype=jnp.float32)
    R = RET_WIDTH
    for l in range(DEPTH):
        h = rmsnorm(x, norm_mix_g[l])
        proj = h @ w_in[l]
        q = rope(proj[..., 0:R].reshape(b, s, RET_HEADS, RET_HEAD_DIM), pos)
        k = rope(proj[..., R:2 * R].reshape(b, s, RET_HEADS, RET_HEAD_DIM), pos)
        v = proj[..., 2 * R:3 * R].reshape(b, s, RET_HEADS, RET_HEAD_DIM)
        g = proj[..., 3 * R:4 * R]
        u = proj[..., 4 * R:]
        y_ret = retention_group(q, k, v, g, ret_gn_g[l])
        y_ssm = s5_group(u, ssm_a_re[l], ssm_a_im[l], ssm_log_dt[l], ssm_b_re[l], ssm_b_im[l],
                         ssm_c_re[l], ssm_c_im[l], ssm_d[l], ssm_w_glu[l], ssm_b_glu[l], ssm_out_g[l])
        mix = jnp.concatenate([y_ret, y_ssm.astype(jnp.float32)], axis=-1).astype(x.dtype)
        x = x + mix @ w_out[l]
        h = rmsnorm(x, norm_ffn_g[l])
        x = x + (jax.nn.silu(h @ w_gate[l]) * (h @ w_up[l])) @ w_down[l]
    return rmsnorm(x, norm_final_g)
```

```python
import functools
import math

import jax
import jax.numpy as jnp
from jax import lax
from jax.experimental import pallas as pl
from jax.experimental.pallas import tpu as pltpu

D_MODEL = 2048
CHUNK = 64
RET_WIDTH = D_MODEL // 2
RET_HEADS = 8
RET_HEAD_DIM = RET_WIDTH // RET_HEADS
SSM_WIDTH = D_MODEL - RET_WIDTH
SSM_GROUP = 16
SSM_GROUPS = SSM_WIDTH // SSM_GROUP
SSM_STATE = 64
D_FF = -(-8 * D_MODEL // (3 * 256)) * 256
IN_WIDTH = 4 * RET_WIDTH + SSM_WIDTH
ROPE_BASE = 10000.0
EPS = 1e-6

F32 = jnp.float32
BF16 = jnp.bfloat16

V7X_VMEM_BYTES = 64 * 1024 * 1024
VMEM_LIMIT = V7X_VMEM_BYTES - 8 * 1024 * 1024

ROW_TILE = 512
IN_COL_TILE = RET_WIDTH
FF_TILE = 512
RET_BLOCK = 256
SSM_T = 16
SSM_CW = SSM_T * SSM_GROUP
SSM_GB = 8


def _params(sem):
    return pltpu.CompilerParams(dimension_semantics=sem, vmem_limit_bytes=VMEM_LIMIT)


def _rms(x, g):
    return x * lax.rsqrt(jnp.mean(x * x, axis=-1, keepdims=True) + EPS) * g


def _inproj_kernel(x_ref, g_ref, w_ref, cos_ref, sin_ref, o_ref, h_ref):
    j = pl.program_id(1)

    @pl.when(j == 0)
    def _():
        h_ref[...] = _rms(x_ref[...], g_ref[...]).astype(BF16)

    acc = jnp.dot(h_ref[...], w_ref[...], preferred_element_type=F32)

    def rope_store(scale):
        cos = cos_ref[...]
        sin = sin_ref[...]
        for h in range(RET_HEADS):
            hs = slice(h * RET_HEAD_DIM, (h + 1) * RET_HEAD_DIM)
            a = acc[:, hs]
            r = pltpu.roll(a, RET_HEAD_DIM // 2, axis=1)
            o = a * cos + r * sin
            if scale is not None:
                o = o * scale
            o_ref[:, hs] = o.astype(BF16)

    @pl.when(j == 0)
    def _():
        rope_store(None)

    @pl.when(j == 1)
    def _():
        rope_store(RET_HEAD_DIM ** -0.5)

    @pl.when(j >= 2)
    def _():
        o_ref[...] = acc.astype(BF16)


def _inproj(x2d, gain, w_bf16, cos_t, sin_t, seq):
    m = x2d.shape[0]
    tiles_per_seq = seq // ROW_TILE
    return pl.pallas_call(
        _inproj_kernel,
        grid=(m // ROW_TILE, IN_WIDTH // IN_COL_TILE),
        in_specs=[
            pl.BlockSpec((ROW_TILE, D_MODEL), lambda i, j: (i, 0)),
            pl.BlockSpec((1, D_MODEL), lambda i, j: (0, 0)),
            pl.BlockSpec((D_MODEL, IN_COL_TILE), lambda i, j: (0, j)),
            pl.BlockSpec((ROW_TILE, RET_HEAD_DIM), lambda i, j: (i % tiles_per_seq, 0)),
            pl.BlockSpec((ROW_TILE, RET_HEAD_DIM), lambda i, j: (i % tiles_per_seq, 0)),
        ],
        out_specs=pl.BlockSpec((ROW_TILE, IN_COL_TILE), lambda i, j: (i, j)),
        out_shape=jax.ShapeDtypeStruct((m, IN_WIDTH), BF16),
        scratch_shapes=[pltpu.VMEM((ROW_TILE, D_MODEL), BF16)],
        compiler_params=_params(("parallel", "arbitrary")),
        name="inproj",
    )(x2d, gain, w_bf16, cos_t, sin_t)


def _retention_kernel(q_ref, k_ref, v_ref, g_ref, dmat_ref, qdec_ref, kdec_ref,
                      gn_ref, o_ref, st_ref, *, block_decay):
    @pl.when(pl.program_id(1) == 0)
    def _():
        st_ref[...] = jnp.zeros_like(st_ref)

    for h in range(RET_HEADS):
        hs = slice(h * RET_HEAD_DIM, (h + 1) * RET_HEAD_DIM)
        q = q_ref[:, hs]
        k = k_ref[:, hs]
        v = v_ref[:, hs]
        s = lax.dot_general(q, k, (((1,), (1,)), ((), ())),
                            preferred_element_type=F32) * dmat_ref[h]
        out = jnp.dot(s.astype(BF16), v, preferred_element_type=F32)
        state = st_ref[h]
        qd = (q.astype(F32) * qdec_ref[h]).astype(BF16)
        out = out + jnp.dot(qd, state.astype(BF16), preferred_element_type=F32)
        kd = (k.astype(F32) * kdec_ref[h]).astype(BF16)
        kv = lax.dot_general(kd, v, (((0,), (0,)), ((), ())),
                             preferred_element_type=F32)
        st_ref[h] = state * block_decay[h] + kv
        mu = jnp.mean(out, axis=-1, keepdims=True)
        cen = out - mu
        var = jnp.mean(cen * cen, axis=-1, keepdims=True)
        normed = cen * lax.rsqrt(var + EPS) * gn_ref[:, hs]
        gate = g_ref[:, hs].astype(F32)
        o_ref[:, hs] = (jax.nn.silu(gate) * normed).astype(BF16)


def _retention_tables():
    hh = jnp.arange(RET_HEADS, dtype=F32)
    log_g = jnp.log1p(-(2.0 ** (-5.0 - hh)))
    idx = jnp.arange(RET_BLOCK, dtype=F32)
    chunk = jnp.arange(RET_BLOCK) // CHUNK
    diff = idx[:, None] - idx[None, :]
    same = chunk[:, None] == chunk[None, :]
    earlier = chunk[None, :] < chunk[:, None]
    dist = jnp.where(same, jnp.abs(diff), diff)
    dmat = jnp.where((same | earlier)[None],
                     jnp.exp(log_g[:, None, None] * dist[None]), 0.0)
    qdec = jnp.exp(log_g[:, None] * (idx + 1.0)[None, :])
    kdec = jnp.exp(log_g[:, None] * (RET_BLOCK - 1.0 - idx)[None, :])
    qdec = jnp.broadcast_to(qdec[:, :, None], (RET_HEADS, RET_BLOCK, RET_HEAD_DIM))
    kdec = jnp.broadcast_to(kdec[:, :, None], (RET_HEADS, RET_BLOCK, RET_HEAD_DIM))
    return dmat.astype(F32), qdec.astype(F32), kdec.astype(F32)


def _retention(proj, gn_gain, batch, seq):
    m = proj.shape[0]
    nblk = seq // RET_BLOCK
    dmat, qdec, kdec = _retention_tables()
    block_decay = tuple(
        math.exp(math.log1p(-(2.0 ** (-5.0 - h))) * RET_BLOCK) for h in range(RET_HEADS))

    def col(c):
        return pl.BlockSpec((RET_BLOCK, RET_WIDTH), lambda b, t, c=c: (b * nblk + t, c))

    def whole(shape):
        return pl.BlockSpec(shape, lambda b, t: (0,) * len(shape))

    return pl.pallas_call(
        functools.partial(_retention_kernel, block_decay=block_decay),
        grid=(batch, nblk),
        in_specs=[
            col(0), col(1), col(2), col(3),
            whole((RET_HEADS, RET_BLOCK, RET_BLOCK)),
            whole((RET_HEADS, RET_BLOCK, RET_HEAD_DIM)),
            whole((RET_HEADS, RET_BLOCK, RET_HEAD_DIM)),
            whole((1, RET_WIDTH)),
        ],
        out_specs=pl.BlockSpec((RET_BLOCK, RET_WIDTH), lambda b, t: (b * nblk + t, 0)),
        out_shape=jax.ShapeDtypeStruct((m, RET_WIDTH), BF16),
        scratch_shapes=[pltpu.VMEM((RET_HEADS, RET_HEAD_DIM, RET_HEAD_DIM), F32)],
        compiler_params=_params(("parallel", "arbitrary")),
        name="retention",
    )(proj, proj, proj, proj, dmat, qdec, kdec, gn_gain)


def _s5_matrices(a_re, a_im, log_dt, b_re, b_im, c_re, c_im, d_skip):
    hp = lax.Precision.HIGHEST
    t = SSM_T
    lam = lax.complex(a_re.astype(F32), a_im.astype(F32))
    dt = jnp.exp(log_dt.astype(F32))[:, None]
    z = lam * dt
    taus = jnp.arange(t + 1, dtype=F32)
    pw = jnp.exp(z[None] * taus[:, None, None])
    lam_bar = pw[1]
    b_c = lax.complex(b_re.astype(F32), b_im.astype(F32))
    b_bar = ((lam_bar - 1.0) / lam)[..., None] * b_c
    c_c = lax.complex(c_re.astype(F32), c_im.astype(F32))

    w_c = jnp.einsum('sgp,gpc->gscp', pw[:t][::-1], b_bar, precision=hp)
    w_c = w_c.reshape(SSM_GROUPS, SSM_CW, SSM_STATE)
    w_re, w_im = jnp.real(w_c), jnp.imag(w_c)
    wmat = jnp.concatenate([w_re, w_im, w_im, w_re], axis=-1)

    kt = jnp.real(jnp.einsum('gop,tgp,gpc->gtco', c_c, pw[:t], b_bar, precision=hp))
    s_idx = jnp.arange(t)[:, None]
    j_idx = jnp.arange(t)[None, :]
    tau = j_idx - s_idx
    blocks = kt[:, jnp.clip(tau, 0, t - 1)]
    blocks = jnp.where((tau >= 0)[None, :, :, None, None], blocks, 0.0)
    d_g = d_skip.astype(F32).reshape(SSM_GROUPS, SSM_GROUP)
    eye_t = jnp.eye(t, dtype=F32)
    eye_c = jnp.eye(SSM_GROUP, dtype=F32)
    blocks = blocks + (eye_t[None, :, :, None, None] * eye_c[None, None, None]
                       * d_g[:, None, None, :, None])
    mmat = blocks.transpose(0, 1, 3, 2, 4).reshape(SSM_GROUPS, SSM_CW, SSM_CW)

    v_c = jnp.einsum('gop,jgp->gpjo', c_c, pw[1:], precision=hp)
    v_c = v_c.reshape(SSM_GROUPS, SSM_STATE, SSM_CW)
    vmat = jnp.concatenate([jnp.real(v_c), -jnp.imag(v_c)], axis=1)

    lam_t = pw[t]
    lr, li = jnp.real(lam_t), jnp.imag(lam_t)
    dec_a = jnp.concatenate([lr, lr], axis=-1)
    dec_b = jnp.concatenate([-li, li], axis=-1)
    return (wmat.astype(BF16), mmat.astype(BF16), vmat.astype(BF16),
            dec_a.astype(F32), dec_b.astype(F32))


def _s5_kernel(u_ref, w_ref, m_ref, v_ref, da_ref, db_ref, y_ref,
               wx_ref, ws_ref, xp_ref, *, n_chunks, batch):
    per_batch = n_chunks // batch
    state_w = 2 * SSM_STATE
    for g in range(SSM_GB):
        rows = slice(g * n_chunks, (g + 1) * n_chunks)
        w = jnp.dot(u_ref[g], w_ref[g], preferred_element_type=F32)
        wx_ref[rows, :] = w[:, :state_w]
        ws_ref[rows, :] = w[:, state_w:]

    dec_a = da_ref[...]
    dec_b = db_ref[...]

    def step(n, carry):
        new = []
        for b in range(batch):
            x, xs = carry[b]
            idx = pl.ds(b * per_batch + n, SSM_GB, stride=n_chunks)
            xp_ref[idx, :] = x
            wx = wx_ref[idx, :]
            ws = ws_ref[idx, :]
            new.append((dec_a * x + dec_b * xs + wx,
                        dec_a * xs - dec_b * x + ws))
        return tuple(new)

    zero = jnp.zeros((SSM_GB, state_w), F32)
    lax.fori_loop(0, per_batch, step, tuple((zero, zero) for _ in range(batch)))

    for g in range(SSM_GB):
        rows = slice(g * n_chunks, (g + 1) * n_chunks)
        y = jnp.dot(u_ref[g], m_ref[g], preferred_element_type=F32)
        y = y + jnp.dot(xp_ref[rows, :].astype(BF16), v_ref[g],
                        preferred_element_type=F32)
        y_ref[g] = y


def _s5(u_chunks, mats, batch):
    wmat, mmat, vmat, dec_a, dec_b = mats
    n_chunks = u_chunks.shape[1]
    state_w = 2 * SSM_STATE

    def grp(shape):
        return pl.BlockSpec((SSM_GB,) + shape, lambda i: (i,) + (0,) * len(shape))

    return pl.pallas_call(
        functools.partial(_s5_kernel, n_chunks=n_chunks, batch=batch),
        grid=(SSM_GROUPS // SSM_GB,),
        in_specs=[
            grp((n_chunks, SSM_CW)),
            grp((SSM_CW, 2 * state_w)),
            grp((SSM_CW, SSM_CW)),
            grp((state_w, SSM_CW)),
            grp((state_w,)),
            grp((state_w,)),
        ],
        out_specs=grp((n_chunks, SSM_CW)),
        out_shape=jax.ShapeDtypeStruct((SSM_GROUPS, n_chunks, SSM_CW), F32),
        scratch_shapes=[
            pltpu.VMEM((SSM_GB * n_chunks, state_w), F32),
            pltpu.VMEM((SSM_GB * n_chunks, state_w), F32),
            pltpu.VMEM((SSM_GB * n_chunks, state_w), F32),
        ],
        compiler_params=_params(("parallel",)),
        name="s5_chunks",
    )(u_chunks, wmat, mmat, vmat, dec_a, dec_b)


def _outproj_kernel(x_ref, yr_ref, ys_ref, wglu_ref, bglu_ref, og_ref, wout_ref, o_ref):
    y1 = jax.nn.gelu(ys_ref[...])
    z = jnp.dot(y1.astype(BF16), wglu_ref[...], preferred_element_type=F32) + bglu_ref[...]
    y2 = y1 * jax.nn.sigmoid(z)
    y_ssm = _rms(y2, og_ref[...]).astype(BF16)
    acc = jnp.dot(yr_ref[...], wout_ref[:RET_WIDTH, :], preferred_element_type=F32)
    acc = acc + jnp.dot(y_ssm, wout_ref[RET_WIDTH:, :], preferred_element_type=F32)
    o_ref[...] = x_ref[...] + acc


def _outproj(x2d, y_ret, y_s5, w_glu, b_glu, out_g, w_out):
    m = x2d.shape[0]

    def rows(width):
        return pl.BlockSpec((ROW_TILE, width), lambda i: (i, 0))

    def whole(shape):
        return pl.BlockSpec(shape, lambda i: (0, 0))

    return pl.pallas_call(
        _outproj_kernel,
        grid=(m // ROW_TILE,),
        in_specs=[
            rows(D_MODEL), rows(RET_WIDTH), rows(SSM_WIDTH),
            whole((SSM_WIDTH, SSM_WIDTH)), whole((1, SSM_WIDTH)), whole((1, SSM_WIDTH)),
            whole((D_MODEL, D_MODEL)),
        ],
        out_specs=rows(D_MODEL),
        out_shape=jax.ShapeDtypeStruct((m, D_MODEL), F32),
        compiler_params=_params(("parallel",)),
        name="outproj",
    )(x2d, y_ret, y_s5, w_glu, b_glu, out_g, w_out)


def _ffn_kernel(x_ref, g_ref, wg_ref, wu_ref, wd_ref, gf_ref, o_ref, h_ref, acc_ref):
    j = pl.program_id(1)

    @pl.when(j == 0)
    def _():
        h_ref[...] = _rms(x_ref[...], g_ref[...]).astype(BF16)
        acc_ref[...] = jnp.zeros_like(acc_ref)

    h = h_ref[...]
    gate = jnp.dot(h, wg_ref[...], preferred_element_type=F32)
    up = jnp.dot(h, wu_ref[...], preferred_element_type=F32)
    act = (jax.nn.silu(gate) * up).astype(BF16)
    acc_ref[...] += jnp.dot(act, wd_ref[...], preferred_element_type=F32)

    @pl.when(j == pl.num_programs(1) - 1)
    def _():
        o_ref[...] = _rms(x_ref[...] + acc_ref[...], gf_ref[...])


def _ffn(x2d, gain, w_gate, w_up, w_down, gain_final):
    m = x2d.shape[0]
    return pl.pallas_call(
        _ffn_kernel,
        grid=(m // ROW_TILE, D_FF // FF_TILE),
        in_specs=[
            pl.BlockSpec((ROW_TILE, D_MODEL), lambda i, j: (i, 0)),
            pl.BlockSpec((1, D_MODEL), lambda i, j: (0, 0)),
            pl.BlockSpec((D_MODEL, FF_TILE), lambda i, j: (0, j)),
            pl.BlockSpec((D_MODEL, FF_TILE), lambda i, j: (0, j)),
            pl.BlockSpec((FF_TILE, D_MODEL), lambda i, j: (j, 0)),
            pl.BlockSpec((1, D_MODEL), lambda i, j: (0, 0)),
        ],
        out_specs=pl.BlockSpec((ROW_TILE, D_MODEL), lambda i, j: (i, 0)),
        out_shape=jax.ShapeDtypeStruct((m, D_MODEL), F32),
        scratch_shapes=[pltpu.VMEM((ROW_TILE, D_MODEL), BF16),
                        pltpu.VMEM((ROW_TILE, D_MODEL), F32)],
        compiler_params=_params(("parallel", "arbitrary")),
        name="ffn",
    )(x2d, gain, w_gate, w_up, w_down, gain_final)


def _rope_tables(seq):
    half = RET_HEAD_DIM // 2
    pos = jnp.arange(seq, dtype=F32)
    freqs = ROPE_BASE ** (-jnp.arange(half, dtype=F32) / half)
    ang = pos[:, None] * freqs[None, :]
    cos = jnp.cos(ang)
    sin = jnp.sin(ang)
    return (jnp.concatenate([cos, cos], axis=-1),
            jnp.concatenate([-sin, sin], axis=-1))


def kernel(x, norm_mix_g, w_in, ret_gn_g, ssm_a_re, ssm_a_im, ssm_log_dt, ssm_b_re, ssm_b_im,
           ssm_c_re, ssm_c_im, ssm_d, ssm_w_glu, ssm_b_glu, ssm_out_g, w_out, norm_ffn_g,
           w_gate, w_up, w_down, norm_final_g):
    batch, seq, d = x.shape
    depth = w_in.shape[0]
    assert d == D_MODEL and seq % ROW_TILE == 0 and seq % RET_BLOCK == 0 and seq % SSM_T == 0
    m = batch * seq
    n_chunks = m // SSM_T
    cos_t, sin_t = _rope_tables(seq)
    x2d = x.reshape(m, d)

    for l in range(depth):
        proj = _inproj(x2d, norm_mix_g[l][None], w_in[l].astype(BF16), cos_t, sin_t, seq)
        y_ret = _retention(proj, ret_gn_g[l][None], batch, seq)

        u = proj[:, 4 * RET_WIDTH:]
        u_chunks = (u.reshape(n_chunks, SSM_T, SSM_GROUPS, SSM_GROUP)
                    .transpose(2, 0, 1, 3).reshape(SSM_GROUPS, n_chunks, SSM_CW))
        mats = _s5_matrices(ssm_a_re[l], ssm_a_im[l], ssm_log_dt[l], ssm_b_re[l], ssm_b_im[l],
                            ssm_c_re[l], ssm_c_im[l], ssm_d[l])
        y_chunks = _s5(u_chunks, mats, batch)
        y_s5 = (y_chunks.reshape(SSM_GROUPS, n_chunks, SSM_T, SSM_GROUP)
                .transpose(1, 2, 0, 3).reshape(m, SSM_WIDTH))

        x2d = _outproj(x2d, y_ret, y_s5, ssm_w_glu[l].astype(BF16), ssm_b_glu[l][None],
                       ssm_out_g[l][None], w_out[l].astype(BF16))
        last = l == depth - 1
        assert last, "fused final norm assumes a single layer"
        x2d = _ffn(x2d, norm_ffn_g[l][None], w_gate[l].astype(BF16), w_up[l].astype(BF16),
                   w_down[l].astype(BF16), norm_final_g[None])
    return x2d.reshape(batch, seq, d)
```

```python
import functools
import math

import jax
import jax.numpy as jnp
from jax import lax
from jax.experimental import pallas as pl
from jax.experimental.pallas import tpu as pltpu

D_MODEL = 2048
CHUNK = 64
RET_WIDTH = D_MODEL // 2
RET_HEADS = 8
RET_HEAD_DIM = RET_WIDTH // RET_HEADS
SSM_WIDTH = D_MODEL - RET_WIDTH
SSM_GROUP = 16
SSM_GROUPS = SSM_WIDTH // SSM_GROUP
SSM_STATE = 64
D_FF = -(-8 * D_MODEL // (3 * 256)) * 256
IN_WIDTH = 4 * RET_WIDTH + SSM_WIDTH
ROPE_BASE = 10000.0
EPS = 1e-6

F32 = jnp.float32
BF16 = jnp.bfloat16

V7X_VMEM_BYTES = 64 * 1024 * 1024
VMEM_LIMIT = V7X_VMEM_BYTES - 8 * 1024 * 1024

ROW_TILE = 512
IN_COL_TILE = RET_WIDTH
FF_TILE = 512
RET_BLOCK = 256
SSM_T = 16
SSM_CW = SSM_T * SSM_GROUP
LANES = 128
SUBLANES = 8
SSM_GB = LANES // SSM_GROUP
SSM_SLABS = SSM_GROUPS // SSM_GB
SSM_SW = 2 * SSM_STATE


def _params(sem):
    return pltpu.CompilerParams(dimension_semantics=sem, vmem_limit_bytes=VMEM_LIMIT)


def _rms(x, g):
    return x * lax.rsqrt(jnp.mean(x * x, axis=-1, keepdims=True) + EPS) * g


def _inproj_kernel(x_ref, g_ref, w_ref, cos_ref, sin_ref, o_ref, us_ref, h_ref):
    j = pl.program_id(1)

    @pl.when(j == 0)
    def _():
        h_ref[...] = _rms(x_ref[...], g_ref[...]).astype(BF16)

    acc = jnp.dot(h_ref[...], w_ref[...], preferred_element_type=F32)

    def rope_store(scale):
        cos = cos_ref[...]
        sin = sin_ref[...]
        for h in range(RET_HEADS):
            hs = slice(h * RET_HEAD_DIM, (h + 1) * RET_HEAD_DIM)
            a = acc[:, hs]
            r = pltpu.roll(a, RET_HEAD_DIM // 2, axis=1)
            o = a * cos + r * sin
            if scale is not None:
                o = o * scale
            o_ref[:, hs] = o.astype(BF16)

    @pl.when(j == 0)
    def _():
        rope_store(None)

    @pl.when(j == 1)
    def _():
        rope_store(RET_HEAD_DIM ** -0.5)

    @pl.when((j == 2) | (j == 3))
    def _():
        o_ref[...] = acc.astype(BF16)

    @pl.when(j == 4)
    def _():
        for k in range(SSM_SLABS):
            us_ref[k] = acc[:, k * LANES:(k + 1) * LANES]


def _inproj(x2d, gain, w_bf16, cos_t, sin_t, seq):
    m = x2d.shape[0]
    tiles_per_seq = seq // ROW_TILE
    n_ret_cols = 4 * RET_WIDTH // IN_COL_TILE
    return pl.pallas_call(
        _inproj_kernel,
        grid=(m // ROW_TILE, IN_WIDTH // IN_COL_TILE),
        in_specs=[
            pl.BlockSpec((ROW_TILE, D_MODEL), lambda i, j: (i, 0)),
            pl.BlockSpec((1, D_MODEL), lambda i, j: (0, 0)),
            pl.BlockSpec((D_MODEL, IN_COL_TILE), lambda i, j: (0, j)),
            pl.BlockSpec((ROW_TILE, RET_HEAD_DIM), lambda i, j: (i % tiles_per_seq, 0)),
            pl.BlockSpec((ROW_TILE, RET_HEAD_DIM), lambda i, j: (i % tiles_per_seq, 0)),
        ],
        out_specs=[
            pl.BlockSpec((ROW_TILE, IN_COL_TILE),
                         lambda i, j: (i, jnp.minimum(j, n_ret_cols - 1))),
            pl.BlockSpec((SSM_SLABS, ROW_TILE, LANES), lambda i, j: (0, i, 0)),
        ],
        out_shape=[jax.ShapeDtypeStruct((m, 4 * RET_WIDTH), BF16),
                   jax.ShapeDtypeStruct((SSM_SLABS, m, LANES), F32)],
        scratch_shapes=[pltpu.VMEM((ROW_TILE, D_MODEL), BF16)],
        compiler_params=_params(("parallel", "arbitrary")),
        name="inproj",
    )(x2d, gain, w_bf16, cos_t, sin_t)


def _retention_kernel(q_ref, k_ref, v_ref, g_ref, dmat_ref, qdec_ref, kdec_ref,
                      gn_ref, o_ref, st_ref, *, block_decay):
    @pl.when(pl.program_id(1) == 0)
    def _():
        st_ref[...] = jnp.zeros_like(st_ref)

    for h in range(RET_HEADS):
        hs = slice(h * RET_HEAD_DIM, (h + 1) * RET_HEAD_DIM)
        q = q_ref[:, hs]
        k = k_ref[:, hs]
        v = v_ref[:, hs]
        s = lax.dot_general(q, k, (((1,), (1,)), ((), ())),
                            preferred_element_type=F32) * dmat_ref[h]
        out = jnp.dot(s.astype(BF16), v, preferred_element_type=F32)
        state = st_ref[h]
        qd = (q.astype(F32) * qdec_ref[h]).astype(BF16)
        out = out + jnp.dot(qd, state.astype(BF16), preferred_element_type=F32)
        kd = (k.astype(F32) * kdec_ref[h]).astype(BF16)
        kv = lax.dot_general(kd, v, (((0,), (0,)), ((), ())),
                             preferred_element_type=F32)
        st_ref[h] = state * block_decay[h] + kv
        mu = jnp.mean(out, axis=-1, keepdims=True)
        cen = out - mu
        var = jnp.mean(cen * cen, axis=-1, keepdims=True)
        normed = cen * lax.rsqrt(var + EPS) * gn_ref[:, hs]
        gate = g_ref[:, hs].astype(F32)
        o_ref[:, hs] = (jax.nn.silu(gate) * normed).astype(BF16)


def _retention_tables():
    hh = jnp.arange(RET_HEADS, dtype=F32)
    log_g = jnp.log1p(-(2.0 ** (-5.0 - hh)))
    idx = jnp.arange(RET_BLOCK, dtype=F32)
    chunk = jnp.arange(RET_BLOCK) // CHUNK
    diff = idx[:, None] - idx[None, :]
    same = chunk[:, None] == chunk[None, :]
    earlier = chunk[None, :] < chunk[:, None]
    dist = jnp.where(same, jnp.abs(diff), diff)
    dmat = jnp.where((same | earlier)[None],
                     jnp.exp(log_g[:, None, None] * dist[None]), 0.0)
    qdec = jnp.exp(log_g[:, None] * (idx + 1.0)[None, :])
    kdec = jnp.exp(log_g[:, None] * (RET_BLOCK - 1.0 - idx)[None, :])
    qdec = jnp.broadcast_to(qdec[:, :, None], (RET_HEADS, RET_BLOCK, RET_HEAD_DIM))
    kdec = jnp.broadcast_to(kdec[:, :, None], (RET_HEADS, RET_BLOCK, RET_HEAD_DIM))
    return dmat.astype(F32), qdec.astype(F32), kdec.astype(F32)


def _retention(proj, gn_gain, batch, seq):
    m = proj.shape[0]
    nblk = seq // RET_BLOCK
    dmat, qdec, kdec = _retention_tables()
    block_decay = tuple(
        math.exp(math.log1p(-(2.0 ** (-5.0 - h))) * RET_BLOCK) for h in range(RET_HEADS))

    def col(c):
        return pl.BlockSpec((RET_BLOCK, RET_WIDTH), lambda b, t, c=c: (b * nblk + t, c))

    def whole(shape):
        return pl.BlockSpec(shape, lambda b, t: (0,) * len(shape))

    return pl.pallas_call(
        functools.partial(_retention_kernel, block_decay=block_decay),
        grid=(batch, nblk),
        in_specs=[
            col(0), col(1), col(2), col(3),
            whole((RET_HEADS, RET_BLOCK, RET_BLOCK)),
            whole((RET_HEADS, RET_BLOCK, RET_HEAD_DIM)),
            whole((RET_HEADS, RET_BLOCK, RET_HEAD_DIM)),
            whole((1, RET_WIDTH)),
        ],
        out_specs=pl.BlockSpec((RET_BLOCK, RET_WIDTH), lambda b, t: (b * nblk + t, 0)),
        out_shape=jax.ShapeDtypeStruct((m, RET_WIDTH), BF16),
        scratch_shapes=[pltpu.VMEM((RET_HEADS, RET_HEAD_DIM, RET_HEAD_DIM), F32)],
        compiler_params=_params(("parallel", "arbitrary")),
        name="retention",
    )(proj, proj, proj, proj, dmat, qdec, kdec, gn_gain)


def _cmul(ar, ai, br, bi):
    return ar * br - ai * bi, ar * bi + ai * br


def _zoh(a_re, a_im, dt):
    e = jnp.exp(a_re * dt)
    lr = e * jnp.cos(a_im * dt)
    li = e * jnp.sin(a_im * dt)
    inv = 1.0 / (a_re * a_re + a_im * a_im)
    xr = lr - 1.0
    return lr, li, (xr * a_re + li * a_im) * inv, (li * a_re - xr * a_im) * inv


def _powers_by_bits(expo, lr, li, n_bits):
    pr = jnp.ones(expo.shape, F32)
    pi = jnp.zeros(expo.shape, F32)
    qr, qi = lr, li
    for bit in range(n_bits):
        sel = (expo & (1 << bit)) != 0
        mr, mi = _cmul(pr, pi, qr, qi)
        pr = jnp.where(sel, mr, pr)
        pi = jnp.where(sel, mi, pi)
        qr, qi = _cmul(qr, qi, qr, qi)
    return pr, pi, qr, qi


def _s5_prep_kernel(logdt_ref, are_l_ref, aim_l_ref, are_s_ref, aim_s_ref, bre_ref, bim_ref,
                    cre_ref, cim_ref, d_ref, w_ref, m_ref, v_ref, da_ref, db_ref):
    n_bits = SSM_T.bit_length() - 1
    dt = jnp.exp(logdt_ref[...])

    lr, li, br, bi = _zoh(are_l_ref[...], aim_l_ref[...], dt)
    bbr, bbi = _cmul(br, bi, bre_ref[...], bim_ref[...])
    low = lax.broadcasted_iota(jnp.int32, (SSM_GROUP, LANES), 1) < SSM_STATE
    x1 = jnp.where(low, bbr, bbi)
    x2 = jnp.where(low, -bbi, bbr)
    row = lax.broadcasted_iota(jnp.int32, (SSM_CW, LANES), 0)
    pr, pi, lr_t, li_t = _powers_by_bits(SSM_T - 1 - row // SSM_GROUP, lr, li, n_bits)
    w = pr * jnp.concatenate([x1] * SSM_T, axis=0) + pi * jnp.concatenate([x2] * SSM_T, axis=0)
    w_ref[...] = w.astype(BF16)
    da_ref[...] = lr_t
    db_ref[...] = jnp.where(low[:1], -li_t, li_t)

    lrs, lis, _, _ = _zoh(are_s_ref[...], aim_s_ref[...], dt)
    tau = lax.broadcasted_iota(jnp.int32, (SSM_STATE, SSM_CW), 1) // SSM_GROUP
    p0r, p0i, _, _ = _powers_by_bits(tau, lrs, lis, n_bits)
    cr, ci = cre_ref[...], cim_ref[...]
    gr, gi = _cmul(cr, ci, p0r, p0i)
    lhs = jnp.where(low, bbr, -bbi)
    k_all = jnp.dot(lhs, jnp.concatenate([gr, gi], axis=0),
                    preferred_element_type=F32, precision=lax.Precision.HIGHEST)
    crow = lax.broadcasted_iota(jnp.int32, (SSM_GROUP, SSM_CW), 0)
    clane = lax.broadcasted_iota(jnp.int32, (SSM_GROUP, SSM_CW), 1)
    k_all = k_all + jnp.where(crow == clane, d_ref[...], 0.0)
    for s in range(SSM_T):
        shifted = k_all if s == 0 else pltpu.roll(k_all, s * SSM_GROUP, axis=1)
        m_ref[s * SSM_GROUP:(s + 1) * SSM_GROUP, :] = jnp.where(
            clane >= s * SSM_GROUP, shifted, 0.0).astype(BF16)
    p1r, p1i = _cmul(p0r, p0i, lrs, lis)
    vr, vi = _cmul(cr, ci, p1r, p1i)
    v_ref[:SSM_STATE, :] = vr.astype(BF16)
    v_ref[SSM_STATE:, :] = (-vi).astype(BF16)


def _s5_prep(a_re, a_im, log_dt, b_re, b_im, c_re, c_im, d_skip):
    g = SSM_GROUPS
    dup = lambda a: jnp.concatenate([a, a], axis=-1)
    args = (
        log_dt.reshape(g, 1, 1),
        dup(a_re)[:, None, :], dup(a_im)[:, None, :],
        a_re[:, :, None], a_im[:, :, None],
        dup(b_re.transpose(0, 2, 1)), dup(b_im.transpose(0, 2, 1)),
        jnp.tile(c_re.transpose(0, 2, 1), (1, 1, SSM_T)),
        jnp.tile(c_im.transpose(0, 2, 1), (1, 1, SSM_T)),
        d_skip.reshape(g, SSM_GROUP, 1),
    )

    def spec(a):
        return pl.BlockSpec((None,) + a.shape[1:], lambda i: (i, 0, 0))

    out_shapes = [
        jax.ShapeDtypeStruct((g, SSM_CW, SSM_SW), BF16),
        jax.ShapeDtypeStruct((g, SSM_CW, SSM_CW), BF16),
        jax.ShapeDtypeStruct((g, SSM_SW, SSM_CW), BF16),
        jax.ShapeDtypeStruct((g, 1, SSM_SW), F32),
        jax.ShapeDtypeStruct((g, 1, SSM_SW), F32),
    ]
    wmat, mmat, vmat, dec_a, dec_b = pl.pallas_call(
        _s5_prep_kernel,
        grid=(g,),
        in_specs=[spec(a) for a in args],
        out_specs=[spec(s) for s in out_shapes],
        out_shape=out_shapes,
        compiler_params=_params(("parallel",)),
        name="s5_prep",
    )(*args)
    return wmat, mmat, vmat, dec_a.reshape(g, SSM_SW), dec_b.reshape(g, SSM_SW)


def _block_transpose(vs, lane_block):
    vs = list(vs)
    d = SSM_GB // 2
    while d:
        hi = (lane_block & d) != 0
        nxt = list(vs)
        for i in range(SSM_GB):
            if i & d:
                continue
            a, b = vs[i], vs[i + d]
            nxt[i] = jnp.where(hi, pltpu.roll(b, d * SSM_GROUP, axis=1), a)
            nxt[i + d] = jnp.where(hi, b, pltpu.roll(a, LANES - d * SSM_GROUP, axis=1))
        vs = nxt
        d //= 2
    return vs


S5_IN_ROWS = 16
S5_OUT_ROWS = SUBLANES


def _s5_kernel(u_ref, w_ref, m_ref, v_ref, da_ref, db_ref, y_ref,
               uc_ref, wx_ref, xp_ref, yc_ref, *, n_chunks):
    halves = SSM_T // SSM_GB
    lane_block = lax.broadcasted_iota(jnp.int32, (1, LANES), 1) // SSM_GROUP

    def relayout_in(i, carry):
        for half in range(halves):
            vs = [u_ref[pl.ds(i * (S5_IN_ROWS * SSM_T) + half * SSM_GB + t, S5_IN_ROWS,
                              stride=SSM_T), :] for t in range(SSM_GB)]
            vs = _block_transpose(vs, lane_block)
            for g in range(SSM_GB):
                uc_ref[g, pl.ds(pl.multiple_of(i * S5_IN_ROWS, S5_IN_ROWS), S5_IN_ROWS),
                       half * LANES:(half + 1) * LANES] = vs[g].astype(BF16)
        return carry

    lax.fori_loop(0, n_chunks // S5_IN_ROWS, relayout_in, 0)

    for g in range(SSM_GB):
        w = jnp.dot(uc_ref[g], w_ref[g], preferred_element_type=F32)
        wx_ref[pl.ds(g, n_chunks, stride=SSM_GB), :] = w

    dec_a = da_ref[...]
    dec_b = db_ref[...]

    def step(n, carry):
        x, xs = carry
        rows = pl.ds(pl.multiple_of(n * SSM_GB, SSM_GB), SSM_GB)
        xp_ref[rows, :] = x
        wx = wx_ref[rows, :]
        ws = pltpu.roll(wx, SSM_STATE, axis=1)
        return dec_a * x + dec_b * xs + wx, dec_a * xs - dec_b * x + ws

    zero = jnp.zeros((SSM_GB, SSM_SW), F32)
    lax.fori_loop(0, n_chunks, step, (zero, zero), unroll=8)

    for g in range(SSM_GB):
        y = jnp.dot(uc_ref[g], m_ref[g], preferred_element_type=F32)
        xp = xp_ref[pl.ds(g, n_chunks, stride=SSM_GB), :].astype(BF16)
        yc_ref[g] = y + jnp.dot(xp, v_ref[g], preferred_element_type=F32)

    def relayout_out(i, carry):
        for half in range(halves):
            vs = [yc_ref[g, pl.ds(pl.multiple_of(i * S5_OUT_ROWS, S5_OUT_ROWS), S5_OUT_ROWS),
                         half * LANES:(half + 1) * LANES] for g in range(SSM_GB)]
            vs = _block_transpose(vs, lane_block)
            for t in range(SSM_GB):
                y_ref[pl.ds(i * (S5_OUT_ROWS * SSM_T) + half * SSM_GB + t, S5_OUT_ROWS,
                            stride=SSM_T), :] = vs[t]
        return carry

    lax.fori_loop(0, n_chunks // S5_OUT_ROWS, relayout_out, 0)


def _s5(u_slabs, mats, batch, seq):
    wmat, mmat, vmat, dec_a, dec_b = mats
    m = u_slabs.shape[1]
    n_chunks = seq // SSM_T

    def grp(shape):
        return pl.BlockSpec((SSM_GB,) + shape, lambda k, b: (k,) + (0,) * len(shape))

    seq_spec = pl.BlockSpec((None, seq, LANES), lambda k, b: (k, b, 0))
    return pl.pallas_call(
        functools.partial(_s5_kernel, n_chunks=n_chunks),
        grid=(SSM_SLABS, batch),
        in_specs=[
            seq_spec,
            grp((SSM_CW, SSM_SW)),
            grp((SSM_CW, SSM_CW)),
            grp((SSM_SW, SSM_CW)),
            grp((SSM_SW,)),
            grp((SSM_SW,)),
        ],
        out_specs=seq_spec,
        out_shape=jax.ShapeDtypeStruct((SSM_SLABS, m, LANES), F32),
        scratch_shapes=[
            pltpu.VMEM((SSM_GB, n_chunks, SSM_CW), BF16),
            pltpu.VMEM((SSM_GB * n_chunks, SSM_SW), F32),
            pltpu.VMEM((SSM_GB * n_chunks, SSM_SW), F32),
            pltpu.VMEM((SSM_GB, n_chunks, SSM_CW), F32),
        ],
        compiler_params=_params(("parallel", "parallel")),
        name="s5_chunks",
    )(u_slabs, wmat, mmat, vmat, dec_a, dec_b)


def _outproj_kernel(x_ref, yr_ref, ys_ref, wglu_ref, bglu_ref, og_ref, wout_ref, o_ref):
    y1 = jax.nn.gelu(jnp.concatenate([ys_ref[k] for k in range(SSM_SLABS)], axis=-1))
    z = jnp.dot(y1.astype(BF16), wglu_ref[...], preferred_element_type=F32) + bglu_ref[...]
    y2 = y1 * jax.nn.sigmoid(z)
    y_ssm = _rms(y2, og_ref[...]).astype(BF16)
    acc = jnp.dot(yr_ref[...], wout_ref[:RET_WIDTH, :], preferred_element_type=F32)
    acc = acc + jnp.dot(y_ssm, wout_ref[RET_WIDTH:, :], preferred_element_type=F32)
    o_ref[...] = x_ref[...] + acc


def _outproj(x2d, y_ret, y_s5, w_glu, b_glu, out_g, w_out):
    m = x2d.shape[0]

    def rows(width):
        return pl.BlockSpec((ROW_TILE, width), lambda i: (i, 0))

    def whole(shape):
        return pl.BlockSpec(shape, lambda i: (0, 0))

    return pl.pallas_call(
        _outproj_kernel,
        grid=(m // ROW_TILE,),
        in_specs=[
            rows(D_MODEL), rows(RET_WIDTH),
            pl.BlockSpec((SSM_SLABS, ROW_TILE, LANES), lambda i: (0, i, 0)),
            whole((SSM_WIDTH, SSM_WIDTH)), whole((1, SSM_WIDTH)), whole((1, SSM_WIDTH)),
            whole((D_MODEL, D_MODEL)),
        ],
        out_specs=rows(D_MODEL),
        out_shape=jax.ShapeDtypeStruct((m, D_MODEL), F32),
        compiler_params=_params(("parallel",)),
        name="outproj",
    )(x2d, y_ret, y_s5, w_glu, b_glu, out_g, w_out)


def _ffn_kernel(x_ref, g_ref, wg_ref, wu_ref, wd_ref, gf_ref, o_ref, h_ref, acc_ref):
    j = pl.program_id(1)

    @pl.when(j == 0)
    def _():
        h_ref[...] = _rms(x_ref[...], g_ref[...]).astype(BF16)
        acc_ref[...] = jnp.zeros_like(acc_ref)

    h = h_ref[...]
    gate = jnp.dot(h, wg_ref[...], preferred_element_type=F32)
    up = jnp.dot(h, wu_ref[...], preferred_element_type=F32)
    act = (jax.nn.silu(gate) * up).astype(BF16)
    acc_ref[...] += jnp.dot(act, wd_ref[...], preferred_element_type=F32)

    @pl.when(j == pl.num_programs(1) - 1)
    def _():
        o_ref[...] = _rms(x_ref[...] + acc_ref[...], gf_ref[...])


def _ffn(x2d, gain, w_gate, w_up, w_down, gain_final):
    m = x2d.shape[0]
    return pl.pallas_call(
        _ffn_kernel,
        grid=(m // ROW_TILE, D_FF // FF_TILE),
        in_specs=[
            pl.BlockSpec((ROW_TILE, D_MODEL), lambda i, j: (i, 0)),
            pl.BlockSpec((1, D_MODEL), lambda i, j: (0, 0)),
            pl.BlockSpec((D_MODEL, FF_TILE), lambda i, j: (0, j)),
            pl.BlockSpec((D_MODEL, FF_TILE), lambda i, j: (0, j)),
            pl.BlockSpec((FF_TILE, D_MODEL), lambda i, j: (j, 0)),
            pl.BlockSpec((1, D_MODEL), lambda i, j: (0, 0)),
        ],
        out_specs=pl.BlockSpec((ROW_TILE, D_MODEL), lambda i, j: (i, 0)),
        out_shape=jax.ShapeDtypeStruct((m, D_MODEL), F32),
        scratch_shapes=[pltpu.VMEM((ROW_TILE, D_MODEL), BF16),
                        pltpu.VMEM((ROW_TILE, D_MODEL), F32)],
        compiler_params=_params(("parallel", "arbitrary")),
        name="ffn",
    )(x2d, gain, w_gate, w_up, w_down, gain_final)


def _rope_tables(seq):
    half = RET_HEAD_DIM // 2
    pos = jnp.arange(seq, dtype=F32)
    freqs = ROPE_BASE ** (-jnp.arange(half, dtype=F32) / half)
    ang = pos[:, None] * freqs[None, :]
    cos = jnp.cos(ang)
    sin = jnp.sin(ang)
    return (jnp.concatenate([cos, cos], axis=-1),
            jnp.concatenate([-sin, sin], axis=-1))


def kernel(x, norm_mix_g, w_in, ret_gn_g, ssm_a_re, ssm_a_im, ssm_log_dt, ssm_b_re, ssm_b_im,
           ssm_c_re, ssm_c_im, ssm_d, ssm_w_glu, ssm_b_glu, ssm_out_g, w_out, norm_ffn_g,
           w_gate, w_up, w_down, norm_final_g):
    batch, seq, d = x.shape
    depth = w_in.shape[0]
    assert d == D_MODEL and seq % ROW_TILE == 0 and seq % RET_BLOCK == 0 and seq % SSM_T == 0
    m = batch * seq
    cos_t, sin_t = _rope_tables(seq)
    x2d = x.reshape(m, d)

    for l in range(depth):
        proj, u_slabs = _inproj(x2d, norm_mix_g[l][None], w_in[l].astype(BF16), cos_t, sin_t, seq)
        y_ret = _retention(proj, ret_gn_g[l][None], batch, seq)
        mats = _s5_prep(ssm_a_re[l], ssm_a_im[l], ssm_log_dt[l], ssm_b_re[l], ssm_b_im[l],
                        ssm_c_re[l], ssm_c_im[l], ssm_d[l])
        y_s5 = _s5(u_slabs, mats, batch, seq)

        x2d = _outproj(x2d, y_ret, y_s5, ssm_w_glu[l].astype(BF16), ssm_b_glu[l][None],
                       ssm_out_g[l][None], w_out[l].astype(BF16))
        last = l == depth - 1
        assert last, "fused final norm assumes a single layer"
        x2d = _ffn(x2d, norm_ffn_g[l][None], w_gate[l].astype(BF16), w_up[l].astype(BF16),
                   w_down[l].astype(BF16), norm_final_g[None])
    return x2d.reshape(batch, seq, d)
```

```python
import functools
import math

import jax
import jax.numpy as jnp
from jax import lax
from jax.experimental import pallas as pl
from jax.experimental.pallas import tpu as pltpu

D_MODEL = 2048
CHUNK = 64
RET_WIDTH = D_MODEL // 2
RET_HEADS = 8
RET_HEAD_DIM = RET_WIDTH // RET_HEADS
SSM_WIDTH = D_MODEL - RET_WIDTH
SSM_GROUP = 16
SSM_GROUPS = SSM_WIDTH // SSM_GROUP
SSM_STATE = 64
D_FF = -(-8 * D_MODEL // (3 * 256)) * 256
IN_WIDTH = 4 * RET_WIDTH + SSM_WIDTH
ROPE_BASE = 10000.0
EPS = 1e-6

F32 = jnp.float32
BF16 = jnp.bfloat16

V7X_VMEM_BYTES = 64 * 1024 * 1024
VMEM_LIMIT = V7X_VMEM_BYTES - 8 * 1024 * 1024

ROW_TILE = 512
IN_COL_TILE = RET_WIDTH
FFN_ROW_TILE = 1024
FF_TILE = 512
RET_BLOCK = 256
SSM_T = 16
SSM_CW = SSM_T * SSM_GROUP
LANES = 128
SUBLANES = 8
SSM_GB = LANES // SSM_GROUP
SSM_SLABS = SSM_GROUPS // SSM_GB
SSM_SW = 2 * SSM_STATE


def _params(sem):
    return pltpu.CompilerParams(dimension_semantics=sem, vmem_limit_bytes=VMEM_LIMIT)


def _rms(x, g):
    return x * lax.rsqrt(jnp.mean(x * x, axis=-1, keepdims=True) + EPS) * g


def _inproj_kernel(x_ref, g_ref, w_ref, cos_ref, sin_ref, o_ref, us_ref, h_ref):
    j = pl.program_id(1)

    @pl.when(j == 0)
    def _():
        h_ref[...] = _rms(x_ref[...], g_ref[...]).astype(BF16)

    acc = jnp.dot(h_ref[...], w_ref[...], preferred_element_type=F32)

    def rope_store(scale):
        cos = cos_ref[...]
        sin = sin_ref[...]
        for h in range(RET_HEADS):
            hs = slice(h * RET_HEAD_DIM, (h + 1) * RET_HEAD_DIM)
            a = acc[:, hs]
            r = pltpu.roll(a, RET_HEAD_DIM // 2, axis=1)
            o = a * cos + r * sin
            if scale is not None:
                o = o * scale
            o_ref[:, hs] = o.astype(BF16)

    @pl.when(j == 0)
    def _():
        rope_store(None)

    @pl.when(j == 1)
    def _():
        rope_store(RET_HEAD_DIM ** -0.5)

    @pl.when((j == 2) | (j == 3))
    def _():
        o_ref[...] = acc.astype(BF16)

    @pl.when(j == 4)
    def _():
        for k in range(SSM_SLABS):
            us_ref[k] = acc[:, k * LANES:(k + 1) * LANES]


def _inproj(x2d, gain, w_bf16, cos_t, sin_t, seq):
    m = x2d.shape[0]
    tiles_per_seq = seq // ROW_TILE
    n_ret_cols = 4 * RET_WIDTH // IN_COL_TILE
    return pl.pallas_call(
        _inproj_kernel,
        grid=(m // ROW_TILE, IN_WIDTH // IN_COL_TILE),
        in_specs=[
            pl.BlockSpec((ROW_TILE, D_MODEL), lambda i, j: (i, 0)),
            pl.BlockSpec((1, D_MODEL), lambda i, j: (0, 0)),
            pl.BlockSpec((D_MODEL, IN_COL_TILE), lambda i, j: (0, j)),
            pl.BlockSpec((ROW_TILE, RET_HEAD_DIM), lambda i, j: (i % tiles_per_seq, 0)),
            pl.BlockSpec((ROW_TILE, RET_HEAD_DIM), lambda i, j: (i % tiles_per_seq, 0)),
        ],
        out_specs=[
            pl.BlockSpec((ROW_TILE, IN_COL_TILE),
                         lambda i, j: (i, jnp.minimum(j, n_ret_cols - 1))),
            pl.BlockSpec((SSM_SLABS, ROW_TILE, LANES), lambda i, j: (0, i, 0)),
        ],
        out_shape=[jax.ShapeDtypeStruct((m, 4 * RET_WIDTH), BF16),
                   jax.ShapeDtypeStruct((SSM_SLABS, m, LANES), F32)],
        scratch_shapes=[pltpu.VMEM((ROW_TILE, D_MODEL), BF16)],
        compiler_params=_params(("parallel", "arbitrary")),
        name="inproj",
    )(x2d, gain, w_bf16, cos_t, sin_t)


def _retention_kernel(q_ref, k_ref, v_ref, g_ref, dmat_ref, qdec_ref, kdec_ref,
                      gn_ref, o_ref, st_ref, *, block_decay):
    @pl.when(pl.program_id(1) == 0)
    def _():
        st_ref[...] = jnp.zeros_like(st_ref)

    for h in range(RET_HEADS):
        hs = slice(h * RET_HEAD_DIM, (h + 1) * RET_HEAD_DIM)
        q = q_ref[:, hs]
        k = k_ref[:, hs]
        v = v_ref[:, hs]
        s = lax.dot_general(q, k, (((1,), (1,)), ((), ())),
                            preferred_element_type=F32) * dmat_ref[h]
        out = jnp.dot(s.astype(BF16), v, preferred_element_type=F32)
        state = st_ref[h]
        qd = (q.astype(F32) * qdec_ref[h]).astype(BF16)
        out = out + jnp.dot(qd, state.astype(BF16), preferred_element_type=F32)
        kd = (k.astype(F32) * kdec_ref[h]).astype(BF16)
        kv = lax.dot_general(kd, v, (((0,), (0,)), ((), ())),
                             preferred_element_type=F32)
        st_ref[h] = state * block_decay[h] + kv
        mu = jnp.mean(out, axis=-1, keepdims=True)
        cen = out - mu
        var = jnp.mean(cen * cen, axis=-1, keepdims=True)
        normed = cen * lax.rsqrt(var + EPS) * gn_ref[:, hs]
        gate = g_ref[:, hs].astype(F32)
        o_ref[:, hs] = (jax.nn.silu(gate) * normed).astype(BF16)


def _retention_tables():
    hh = jnp.arange(RET_HEADS, dtype=F32)
    log_g = jnp.log1p(-(2.0 ** (-5.0 - hh)))
    idx = jnp.arange(RET_BLOCK, dtype=F32)
    chunk = jnp.arange(RET_BLOCK) // CHUNK
    diff = idx[:, None] - idx[None, :]
    same = chunk[:, None] == chunk[None, :]
    earlier = chunk[None, :] < chunk[:, None]
    dist = jnp.where(same, jnp.abs(diff), diff)
    dmat = jnp.where((same | earlier)[None],
                     jnp.exp(log_g[:, None, None] * dist[None]), 0.0)
    qdec = jnp.exp(log_g[:, None] * (idx + 1.0)[None, :])
    kdec = jnp.exp(log_g[:, None] * (RET_BLOCK - 1.0 - idx)[None, :])
    qdec = jnp.broadcast_to(qdec[:, :, None], (RET_HEADS, RET_BLOCK, RET_HEAD_DIM))
    kdec = jnp.broadcast_to(kdec[:, :, None], (RET_HEADS, RET_BLOCK, RET_HEAD_DIM))
    return dmat.astype(F32), qdec.astype(F32), kdec.astype(F32)


def _retention(proj, gn_gain, batch, seq):
    m = proj.shape[0]
    nblk = seq // RET_BLOCK
    dmat, qdec, kdec = _retention_tables()
    block_decay = tuple(
        math.exp(math.log1p(-(2.0 ** (-5.0 - h))) * RET_BLOCK) for h in range(RET_HEADS))

    def col(c):
        return pl.BlockSpec((RET_BLOCK, RET_WIDTH), lambda b, t, c=c: (b * nblk + t, c))

    def whole(shape):
        return pl.BlockSpec(shape, lambda b, t: (0,) * len(shape))

    return pl.pallas_call(
        functools.partial(_retention_kernel, block_decay=block_decay),
        grid=(batch, nblk),
        in_specs=[
            col(0), col(1), col(2), col(3),
            whole((RET_HEADS, RET_BLOCK, RET_BLOCK)),
            whole((RET_HEADS, RET_BLOCK, RET_HEAD_DIM)),
            whole((RET_HEADS, RET_BLOCK, RET_HEAD_DIM)),
            whole((1, RET_WIDTH)),
        ],
        out_specs=pl.BlockSpec((RET_BLOCK, RET_WIDTH), lambda b, t: (b * nblk + t, 0)),
        out_shape=jax.ShapeDtypeStruct((m, RET_WIDTH), BF16),
        scratch_shapes=[pltpu.VMEM((RET_HEADS, RET_HEAD_DIM, RET_HEAD_DIM), F32)],
        compiler_params=_params(("parallel", "arbitrary")),
        name="retention",
    )(proj, proj, proj, proj, dmat, qdec, kdec, gn_gain)


def _cmul(ar, ai, br, bi):
    return ar * br - ai * bi, ar * bi + ai * br


def _zoh(a_re, a_im, dt):
    e = jnp.exp(a_re * dt)
    lr = e * jnp.cos(a_im * dt)
    li = e * jnp.sin(a_im * dt)
    inv = 1.0 / (a_re * a_re + a_im * a_im)
    xr = lr - 1.0
    return lr, li, (xr * a_re + li * a_im) * inv, (li * a_re - xr * a_im) * inv


def _powers_by_bits(expo, lr, li, n_bits):
    pr = jnp.ones(expo.shape, F32)
    pi = jnp.zeros(expo.shape, F32)
    qr, qi = lr, li
    for bit in range(n_bits):
        sel = (expo & (1 << bit)) != 0
        mr, mi = _cmul(pr, pi, qr, qi)
        pr = jnp.where(sel, mr, pr)
        pi = jnp.where(sel, mi, pi)
        qr, qi = _cmul(qr, qi, qr, qi)
    return pr, pi, qr, qi


def _s5_prep_kernel(logdt_ref, are_l_ref, aim_l_ref, are_s_ref, aim_s_ref, bre_ref, bim_ref,
                    cre_ref, cim_ref, d_ref, w_ref, m_ref, v_ref, da_ref, db_ref):
    n_bits = SSM_T.bit_length() - 1
    dt = jnp.exp(logdt_ref[...])

    lr, li, br, bi = _zoh(are_l_ref[...], aim_l_ref[...], dt)
    bbr, bbi = _cmul(br, bi, bre_ref[...], bim_ref[...])
    low = lax.broadcasted_iota(jnp.int32, (SSM_GROUP, LANES), 1) < SSM_STATE
    x1 = jnp.where(low, bbr, bbi)
    x2 = jnp.where(low, -bbi, bbr)
    row = lax.broadcasted_iota(jnp.int32, (SSM_CW, LANES), 0)
    pr, pi, lr_t, li_t = _powers_by_bits(SSM_T - 1 - row // SSM_GROUP, lr, li, n_bits)
    w = pr * jnp.concatenate([x1] * SSM_T, axis=0) + pi * jnp.concatenate([x2] * SSM_T, axis=0)
    w_ref[:, :SSM_SW] = w.astype(BF16)
    w_ref[:, SSM_SW:] = pltpu.roll(w, SSM_STATE, axis=1).astype(BF16)
    da_ref[...] = lr_t
    db_ref[...] = jnp.where(low[:1], -li_t, li_t)

    lrs, lis, _, _ = _zoh(are_s_ref[...], aim_s_ref[...], dt)
    tau = lax.broadcasted_iota(jnp.int32, (SSM_STATE, SSM_CW), 1) // SSM_GROUP
    p0r, p0i, _, _ = _powers_by_bits(tau, lrs, lis, n_bits)
    cr, ci = cre_ref[...], cim_ref[...]
    gr, gi = _cmul(cr, ci, p0r, p0i)
    lhs = jnp.where(low, bbr, -bbi)
    k_all = jnp.dot(lhs, jnp.concatenate([gr, gi], axis=0),
                    preferred_element_type=F32, precision=lax.Precision.HIGHEST)
    crow = lax.broadcasted_iota(jnp.int32, (SSM_GROUP, SSM_CW), 0)
    clane = lax.broadcasted_iota(jnp.int32, (SSM_GROUP, SSM_CW), 1)
    k_all = k_all + jnp.where(crow == clane, d_ref[...], 0.0)
    for s in range(SSM_T):
        shifted = k_all if s == 0 else pltpu.roll(k_all, s * SSM_GROUP, axis=1)
        m_ref[s * SSM_GROUP:(s + 1) * SSM_GROUP, :] = jnp.where(
            clane >= s * SSM_GROUP, shifted, 0.0).astype(BF16)
    p1r, p1i = _cmul(p0r, p0i, lrs, lis)
    vr, vi = _cmul(cr, ci, p1r, p1i)
    v_ref[:SSM_STATE, :] = vr.astype(BF16)
    v_ref[SSM_STATE:, :] = (-vi).astype(BF16)


def _s5_prep(a_re, a_im, log_dt, b_re, b_im, c_re, c_im, d_skip):
    g = SSM_GROUPS
    dup = lambda a: jnp.concatenate([a, a], axis=-1)
    args = (
        log_dt.reshape(g, 1, 1),
        dup(a_re)[:, None, :], dup(a_im)[:, None, :],
        a_re[:, :, None], a_im[:, :, None],
        dup(b_re.transpose(0, 2, 1)), dup(b_im.transpose(0, 2, 1)),
        jnp.tile(c_re.transpose(0, 2, 1), (1, 1, SSM_T)),
        jnp.tile(c_im.transpose(0, 2, 1), (1, 1, SSM_T)),
        d_skip.reshape(g, SSM_GROUP, 1),
    )

    def spec(a):
        return pl.BlockSpec((None,) + a.shape[1:], lambda i: (i, 0, 0))

    out_shapes = [
        jax.ShapeDtypeStruct((g, SSM_CW, 2 * SSM_SW), BF16),
        jax.ShapeDtypeStruct((g, SSM_CW, SSM_CW), BF16),
        jax.ShapeDtypeStruct((g, SSM_SW, SSM_CW), BF16),
        jax.ShapeDtypeStruct((g, 1, SSM_SW), F32),
        jax.ShapeDtypeStruct((g, 1, SSM_SW), F32),
    ]
    wmat, mmat, vmat, dec_a, dec_b = pl.pallas_call(
        _s5_prep_kernel,
        grid=(g,),
        in_specs=[spec(a) for a in args],
        out_specs=[spec(s) for s in out_shapes],
        out_shape=out_shapes,
        compiler_params=_params(("parallel",)),
        name="s5_prep",
    )(*args)
    return wmat, mmat, vmat, dec_a.reshape(g, SSM_SW), dec_b.reshape(g, SSM_SW)


def _block_transpose(vs, lane_block):
    vs = list(vs)
    d = SSM_GB // 2
    while d:
        hi = (lane_block & d) != 0
        nxt = list(vs)
        for i in range(SSM_GB):
            if i & d:
                continue
            a, b = vs[i], vs[i + d]
            nxt[i] = jnp.where(hi, pltpu.roll(b, d * SSM_GROUP, axis=1), a)
            nxt[i + d] = jnp.where(hi, b, pltpu.roll(a, LANES - d * SSM_GROUP, axis=1))
        vs = nxt
        d //= 2
    return vs


S5_IN_ROWS = 16
S5_OUT_ROWS = SUBLANES
S5_UNROLL_IN = 2
S5_UNROLL_OUT = 4


def _s5_kernel(u_ref, w_ref, m_ref, v_ref, da_ref, db_ref, y_ref,
               uc_ref, wx_ref, ws_ref, xp_ref, yc_ref, *, n_chunks):
    halves = SSM_T // SSM_GB
    lane_block = lax.broadcasted_iota(jnp.int32, (1, LANES), 1) // SSM_GROUP

    def relayout_in(i, carry):
        for half in range(halves):
            vs = [u_ref[pl.ds(i * (S5_IN_ROWS * SSM_T) + half * SSM_GB + t, S5_IN_ROWS,
                              stride=SSM_T), :] for t in range(SSM_GB)]
            vs = _block_transpose(vs, lane_block)
            for g in range(SSM_GB):
                uc_ref[g, pl.ds(pl.multiple_of(i * S5_IN_ROWS, S5_IN_ROWS), S5_IN_ROWS),
                       half * LANES:(half + 1) * LANES] = vs[g].astype(BF16)
        return carry

    lax.fori_loop(0, n_chunks // S5_IN_ROWS, relayout_in, 0, unroll=S5_UNROLL_IN)

    for g in range(SSM_GB):
        w = jnp.dot(uc_ref[g], w_ref[g], preferred_element_type=F32)
        wx_ref[pl.ds(g, n_chunks, stride=SSM_GB), :] = w[:, :SSM_SW]
        ws_ref[pl.ds(g, n_chunks, stride=SSM_GB), :] = w[:, SSM_SW:]

    dec_a = da_ref[...]
    dec_b = db_ref[...]

    def step(n, carry):
        x, xs = carry
        rows = pl.ds(pl.multiple_of(n * SSM_GB, SSM_GB), SSM_GB)
        xp_ref[rows, :] = x
        return (dec_a * x + dec_b * xs + wx_ref[rows, :],
                dec_a * xs - dec_b * x + ws_ref[rows, :])

    zero = jnp.zeros((SSM_GB, SSM_SW), F32)
    lax.fori_loop(0, n_chunks, step, (zero, zero), unroll=8)

    for g in range(SSM_GB):
        y = jnp.dot(uc_ref[g], m_ref[g], preferred_element_type=F32)
        xp = xp_ref[pl.ds(g, n_chunks, stride=SSM_GB), :].astype(BF16)
        yc_ref[g] = y + jnp.dot(xp, v_ref[g], preferred_element_type=F32)

    def relayout_out(i, carry):
        for half in range(halves):
            vs = [yc_ref[g, pl.ds(pl.multiple_of(i * S5_OUT_ROWS, S5_OUT_ROWS), S5_OUT_ROWS),
                         half * LANES:(half + 1) * LANES] for g in range(SSM_GB)]
            vs = _block_transpose(vs, lane_block)
            for t in range(SSM_GB):
                y_ref[pl.ds(i * (S5_OUT_ROWS * SSM_T) + half * SSM_GB + t, S5_OUT_ROWS,
                            stride=SSM_T), :] = vs[t]
        return carry

    lax.fori_loop(0, n_chunks // S5_OUT_ROWS, relayout_out, 0, unroll=S5_UNROLL_OUT)


def _s5(u_slabs, mats, batch, seq):
    wmat, mmat, vmat, dec_a, dec_b = mats
    m = u_slabs.shape[1]
    n_chunks = seq // SSM_T

    def grp(shape):
        return pl.BlockSpec((SSM_GB,) + shape, lambda k, b: (k,) + (0,) * len(shape))

    seq_spec = pl.BlockSpec((None, seq, LANES), lambda k, b: (k, b, 0))
    return pl.pallas_call(
        functools.partial(_s5_kernel, n_chunks=n_chunks),
        grid=(SSM_SLABS, batch),
        in_specs=[
            seq_spec,
            grp((SSM_CW, 2 * SSM_SW)),
            grp((SSM_CW, SSM_CW)),
            grp((SSM_SW, SSM_CW)),
            grp((SSM_SW,)),
            grp((SSM_SW,)),
        ],
        out_specs=seq_spec,
        out_shape=jax.ShapeDtypeStruct((SSM_SLABS, m, LANES), F32),
        scratch_shapes=[
            pltpu.VMEM((SSM_GB, n_chunks, SSM_CW), BF16),
            pltpu.VMEM((SSM_GB * n_chunks, SSM_SW), F32),
            pltpu.VMEM((SSM_GB * n_chunks, SSM_SW), F32),
            pltpu.VMEM((SSM_GB * n_chunks, SSM_SW), F32),
            pltpu.VMEM((SSM_GB, n_chunks, SSM_CW), F32),
        ],
        compiler_params=_params(("parallel", "parallel")),
        name="s5_chunks",
    )(u_slabs, wmat, mmat, vmat, dec_a, dec_b)


def _outproj_kernel(x_ref, yr_ref, ys_ref, wglu_ref, bglu_ref, og_ref, wout_ref, o_ref):
    y1 = jax.nn.gelu(jnp.concatenate([ys_ref[k] for k in range(SSM_SLABS)], axis=-1))
    z = jnp.dot(y1.astype(BF16), wglu_ref[...], preferred_element_type=F32) + bglu_ref[...]
    y2 = y1 * jax.nn.sigmoid(z)
    y_ssm = _rms(y2, og_ref[...]).astype(BF16)
    acc = jnp.dot(yr_ref[...], wout_ref[:RET_WIDTH, :], preferred_element_type=F32)
    acc = acc + jnp.dot(y_ssm, wout_ref[RET_WIDTH:, :], preferred_element_type=F32)
    o_ref[...] = x_ref[...] + acc


def _outproj(x2d, y_ret, y_s5, w_glu, b_glu, out_g, w_out):
    m = x2d.shape[0]

    def rows(width):
        return pl.BlockSpec((ROW_TILE, width), lambda i: (i, 0))

    def whole(shape):
        return pl.BlockSpec(shape, lambda i: (0, 0))

    return pl.pallas_call(
        _outproj_kernel,
        grid=(m // ROW_TILE,),
        in_specs=[
            rows(D_MODEL), rows(RET_WIDTH),
            pl.BlockSpec((SSM_SLABS, ROW_TILE, LANES), lambda i: (0, i, 0)),
            whole((SSM_WIDTH, SSM_WIDTH)), whole((1, SSM_WIDTH)), whole((1, SSM_WIDTH)),
            whole((D_MODEL, D_MODEL)),
        ],
        out_specs=rows(D_MODEL),
        out_shape=jax.ShapeDtypeStruct((m, D_MODEL), F32),
        compiler_params=_params(("parallel",)),
        name="outproj",
    )(x2d, y_ret, y_s5, w_glu, b_glu, out_g, w_out)


def _ffn_kernel(x_ref, g_ref, wg_ref, wu_ref, wd_ref, gf_ref, o_ref, h_ref):
    j = pl.program_id(1)

    @pl.when(j == 0)
    def _():
        x = x_ref[...]
        h_ref[...] = _rms(x, g_ref[...]).astype(BF16)
        o_ref[...] = x

    h = h_ref[...]
    gate = jnp.dot(h, wg_ref[...], preferred_element_type=F32)
    up = jnp.dot(h, wu_ref[...], preferred_element_type=F32)
    act = (jax.nn.silu(gate) * up).astype(BF16)
    o_ref[...] += jnp.dot(act, wd_ref[...], preferred_element_type=F32)

    @pl.when(j == pl.num_programs(1) - 1)
    def _():
        o_ref[...] = _rms(o_ref[...], gf_ref[...])


def _ffn(x2d, gain, w_gate, w_up, w_down, gain_final):
    m = x2d.shape[0]
    return pl.pallas_call(
        _ffn_kernel,
        grid=(m // FFN_ROW_TILE, D_FF // FF_TILE),
        in_specs=[
            pl.BlockSpec((FFN_ROW_TILE, D_MODEL), lambda i, j: (i, 0)),
            pl.BlockSpec((1, D_MODEL), lambda i, j: (0, 0)),
            pl.BlockSpec((D_MODEL, FF_TILE), lambda i, j: (0, j)),
            pl.BlockSpec((D_MODEL, FF_TILE), lambda i, j: (0, j)),
            pl.BlockSpec((FF_TILE, D_MODEL), lambda i, j: (j, 0)),
            pl.BlockSpec((1, D_MODEL), lambda i, j: (0, 0)),
        ],
        out_specs=pl.BlockSpec((FFN_ROW_TILE, D_MODEL), lambda i, j: (i, 0)),
        out_shape=jax.ShapeDtypeStruct((m, D_MODEL), F32),
        scratch_shapes=[pltpu.VMEM((FFN_ROW_TILE, D_MODEL), BF16)],
        compiler_params=_params(("parallel", "arbitrary")),
        name="ffn",
    )(x2d, gain, w_gate, w_up, w_down, gain_final)


def _rope_tables(seq):
    half = RET_HEAD_DIM // 2
    pos = jnp.arange(seq, dtype=F32)
    freqs = ROPE_BASE ** (-jnp.arange(half, dtype=F32) / half)
    ang = pos[:, None] * freqs[None, :]
    cos = jnp.cos(ang)
    sin = jnp.sin(ang)
    return (jnp.concatenate([cos, cos], axis=-1),
            jnp.concatenate([-sin, sin], axis=-1))


def kernel(x, norm_mix_g, w_in, ret_gn_g, ssm_a_re, ssm_a_im, ssm_log_dt, ssm_b_re, ssm_b_im,
           ssm_c_re, ssm_c_im, ssm_d, ssm_w_glu, ssm_b_glu, ssm_out_g, w_out, norm_ffn_g,
           w_gate, w_up, w_down, norm_final_g):
    batch, seq, d = x.shape
    depth = w_in.shape[0]
    assert d == D_MODEL and seq % ROW_TILE == 0 and seq % RET_BLOCK == 0 and seq % SSM_T == 0
    m = batch * seq
    cos_t, sin_t = _rope_tables(seq)
    x2d = x.reshape(m, d)

    for l in range(depth):
        proj, u_slabs = _inproj(x2d, norm_mix_g[l][None], w_in[l].astype(BF16), cos_t, sin_t, seq)
        y_ret = _retention(proj, ret_gn_g[l][None], batch, seq)
        mats = _s5_prep(ssm_a_re[l], ssm_a_im[l], ssm_log_dt[l], ssm_b_re[l], ssm_b_im[l],
                        ssm_c_re[l], ssm_c_im[l], ssm_d[l])
        y_s5 = _s5(u_slabs, mats, batch, seq)

        x2d = _outproj(x2d, y_ret, y_s5, ssm_w_glu[l].astype(BF16), ssm_b_glu[l][None],
                       ssm_out_g[l][None], w_out[l].astype(BF16))
        last = l == depth - 1
        assert last, "fused final norm assumes a single layer"
        x2d = _ffn(x2d, norm_ffn_g[l][None], w_gate[l].astype(BF16), w_up[l].astype(BF16),
                   w_down[l].astype(BF16), norm_final_g[None])
    return x2d.reshape(batch, seq, d)
```

```python
import functools
import math

import jax
import jax.numpy as jnp
import numpy as np
from jax import lax
from jax.experimental import pallas as pl
from jax.experimental.pallas import tpu as pltpu

D_MODEL = 2048
CHUNK = 64
RET_WIDTH = D_MODEL // 2
RET_HEADS = 8
RET_HEAD_DIM = RET_WIDTH // RET_HEADS
SSM_WIDTH = D_MODEL - RET_WIDTH
SSM_GROUP = 16
SSM_GROUPS = SSM_WIDTH // SSM_GROUP
SSM_STATE = 64
D_FF = -(-8 * D_MODEL // (3 * 256)) * 256
IN_WIDTH = 4 * RET_WIDTH + SSM_WIDTH
ROPE_BASE = 10000.0
EPS = 1e-6

F32 = jnp.float32
BF16 = jnp.bfloat16

V7X_VMEM_BYTES = 64 * 1024 * 1024
VMEM_LIMIT = V7X_VMEM_BYTES - 8 * 1024 * 1024

ROW_TILE = 512
IN_COL_TILE = RET_WIDTH
FFN_ROW_TILE = 1024
FF_TILE = 512
RET_BLOCK = 256
SSM_T = 16
SSM_CW = SSM_T * SSM_GROUP
LANES = 128
SUBLANES = 8
SSM_GB = LANES // SSM_GROUP
SSM_SLABS = SSM_GROUPS // SSM_GB
SSM_SW = 2 * SSM_STATE


def _params(sem):
    return pltpu.CompilerParams(dimension_semantics=sem, vmem_limit_bytes=VMEM_LIMIT)


def _rms(x, g):
    return x * lax.rsqrt(jnp.mean(x * x, axis=-1, keepdims=True) + EPS) * g


def _inproj_kernel(x_ref, g_ref, w_ref, *rest, mode):
    if mode == "rope":
        cos_ref, sin_ref, o_ref, wb_ref = rest
    else:
        o_ref, wb_ref = rest

    @pl.when(pl.program_id(1) == 0)
    def _():
        wb_ref[...] = w_ref[...].astype(BF16)

    x = x_ref[...]
    inv = lax.rsqrt(jnp.mean(x * x, axis=-1, keepdims=True) + EPS)
    acc = jnp.dot((x * g_ref[...]).astype(BF16), wb_ref[...], preferred_element_type=F32)
    if mode == "rope":
        cos = cos_ref[...] * inv
        sin = sin_ref[...] * inv
        for h in range(RET_HEADS):
            hs = slice(h * RET_HEAD_DIM, (h + 1) * RET_HEAD_DIM)
            a = acc[:, hs]
            o_ref[:, hs] = (a * cos + pltpu.roll(a, RET_HEAD_DIM // 2, axis=1) * sin).astype(BF16)
    elif mode == "plain":
        o_ref[...] = (acc * inv).astype(BF16)
    else:
        acc = acc * inv
        for k in range(SSM_SLABS):
            o_ref[k] = acc[:, k * LANES:(k + 1) * LANES]


def _inproj(x2d, gain, w_in, col0, n_cols, mode, rope=None, seq=None):
    m = x2d.shape[0]
    in_specs = [
        pl.BlockSpec((ROW_TILE, D_MODEL), lambda j, i: (i, 0)),
        pl.BlockSpec((1, D_MODEL), lambda j, i: (0, 0)),
        pl.BlockSpec((D_MODEL, IN_COL_TILE), lambda j, i: (0, col0 + j)),
    ]
    args = [x2d, gain, w_in]
    if mode == "rope":
        tiles_per_seq = seq // ROW_TILE
        tab = pl.BlockSpec((None, ROW_TILE, RET_HEAD_DIM),
                           lambda j, i: (j, i % tiles_per_seq, 0))
        in_specs += [tab, tab]
        args += list(rope)
    if mode == "slabs":
        assert n_cols == 1
        out_spec = pl.BlockSpec((SSM_SLABS, ROW_TILE, LANES), lambda j, i: (0, i, 0))
        out_shape = jax.ShapeDtypeStruct((SSM_SLABS, m, LANES), F32)
    else:
        out_spec = pl.BlockSpec((ROW_TILE, IN_COL_TILE), lambda j, i: (i, j))
        out_shape = jax.ShapeDtypeStruct((m, n_cols * IN_COL_TILE), BF16)
    return pl.pallas_call(
        functools.partial(_inproj_kernel, mode=mode),
        grid=(n_cols, m // ROW_TILE),
        in_specs=in_specs,
        out_specs=out_spec,
        out_shape=out_shape,
        scratch_shapes=[pltpu.VMEM((D_MODEL, IN_COL_TILE), BF16)],
        compiler_params=_params(("arbitrary", "arbitrary")),
        name="inproj_" + mode,
    )(*args)


def _retention_kernel(q_ref, k_ref, v_ref, g_ref, dmat_ref, qdec_ref, kdec_ref,
                      gn_ref, o_ref, st_ref, *, block_decay):
    @pl.when(pl.program_id(1) == 0)
    def _():
        st_ref[...] = jnp.zeros_like(st_ref)

    for h in range(RET_HEADS):
        hs = slice(h * RET_HEAD_DIM, (h + 1) * RET_HEAD_DIM)
        q = q_ref[:, hs]
        k = k_ref[:, hs]
        v = v_ref[:, hs]
        s = lax.dot_general(q, k, (((1,), (1,)), ((), ())),
                            preferred_element_type=F32) * dmat_ref[h]
        out = jnp.dot(s.astype(BF16), v, preferred_element_type=F32)
        state = st_ref[h]
        qd = (q.astype(F32) * qdec_ref[h]).astype(BF16)
        out = out + jnp.dot(qd, state.astype(BF16), preferred_element_type=F32)
        kd = (k.astype(F32) * kdec_ref[h]).astype(BF16)
        kv = lax.dot_general(kd, v, (((0,), (0,)), ((), ())),
                             preferred_element_type=F32)
        st_ref[h] = state * block_decay[h] + kv
        mu = jnp.mean(out, axis=-1, keepdims=True)
        cen = out - mu
        var = jnp.mean(cen * cen, axis=-1, keepdims=True)
        normed = cen * lax.rsqrt(var + EPS) * gn_ref[:, hs]
        gate = g_ref[:, hs].astype(F32)
        o_ref[:, hs] = (jax.nn.silu(gate) * normed).astype(BF16)


def _retention_log_decay():
    return np.log1p(-(2.0 ** (-5.0 - np.arange(RET_HEADS, dtype=np.float64))))


def _retention_tables():
    log_g = _retention_log_decay()
    idx = np.arange(RET_BLOCK, dtype=np.float64)
    chunk = np.arange(RET_BLOCK) // CHUNK
    diff = idx[:, None] - idx[None, :]
    same = chunk[:, None] == chunk[None, :]
    earlier = chunk[None, :] < chunk[:, None]
    dist = np.where(same, np.abs(diff), diff)
    dmat = np.where((same | earlier)[None], np.exp(log_g[:, None, None] * dist[None]), 0.0)
    qdec = np.exp(log_g[:, None] * (idx + 1.0)[None, :])
    kdec = np.exp(log_g[:, None] * (RET_BLOCK - 1.0 - idx)[None, :])
    qdec = np.broadcast_to(qdec[:, :, None], (RET_HEADS, RET_BLOCK, RET_HEAD_DIM))
    kdec = np.broadcast_to(kdec[:, :, None], (RET_HEADS, RET_BLOCK, RET_HEAD_DIM))
    return tuple(jnp.asarray(a, dtype=F32) for a in (dmat, qdec, kdec))


def _retention(qk, vg, gn_gain, batch, seq):
    m = qk.shape[0]
    nblk = seq // RET_BLOCK
    dmat, qdec, kdec = _retention_tables()
    block_decay = tuple(float(v) for v in np.exp(_retention_log_decay() * RET_BLOCK))

    def col(c):
        return pl.BlockSpec((RET_BLOCK, RET_WIDTH), lambda b, t, c=c: (b * nblk + t, c))

    def whole(shape):
        return pl.BlockSpec(shape, lambda b, t: (0,) * len(shape))

    return pl.pallas_call(
        functools.partial(_retention_kernel, block_decay=block_decay),
        grid=(batch, nblk),
        in_specs=[
            col(0), col(1), col(0), col(1),
            whole((RET_HEADS, RET_BLOCK, RET_BLOCK)),
            whole((RET_HEADS, RET_BLOCK, RET_HEAD_DIM)),
            whole((RET_HEADS, RET_BLOCK, RET_HEAD_DIM)),
            whole((1, RET_WIDTH)),
        ],
        out_specs=pl.BlockSpec((RET_BLOCK, RET_WIDTH), lambda b, t: (b * nblk + t, 0)),
        out_shape=jax.ShapeDtypeStruct((m, RET_WIDTH), BF16),
        scratch_shapes=[pltpu.VMEM((RET_HEADS, RET_HEAD_DIM, RET_HEAD_DIM), F32)],
        compiler_params=_params(("parallel", "arbitrary")),
        name="retention",
    )(qk, qk, vg, vg, dmat, qdec, kdec, gn_gain)


def _cmul(ar, ai, br, bi):
    return ar * br - ai * bi, ar * bi + ai * br


def _zoh(a_re, a_im, dt):
    e = jnp.exp(a_re * dt)
    lr = e * jnp.cos(a_im * dt)
    li = e * jnp.sin(a_im * dt)
    inv = 1.0 / (a_re * a_re + a_im * a_im)
    xr = lr - 1.0
    return lr, li, (xr * a_re + li * a_im) * inv, (li * a_re - xr * a_im) * inv


def _powers_by_bits(expo, lr, li, n_bits):
    pr = jnp.ones(expo.shape, F32)
    pi = jnp.zeros(expo.shape, F32)
    qr, qi = lr, li
    for bit in range(n_bits):
        sel = (expo & (1 << bit)) != 0
        mr, mi = _cmul(pr, pi, qr, qi)
        pr = jnp.where(sel, mr, pr)
        pi = jnp.where(sel, mi, pi)
        qr, qi = _cmul(qr, qi, qr, qi)
    return pr, pi, qr, qi


def _s5_prep_kernel(logdt_ref, are_l_ref, aim_l_ref, are_s_ref, aim_s_ref, bre_ref, bim_ref,
                    cre_ref, cim_ref, d_ref, w_ref, m_ref, v_ref, da_ref, db_ref):
    n_bits = SSM_T.bit_length() - 1
    dt = jnp.exp(logdt_ref[...])

    lr, li, br, bi = _zoh(are_l_ref[...], aim_l_ref[...], dt)
    bbr, bbi = _cmul(br, bi, bre_ref[...], bim_ref[...])
    low = lax.broadcasted_iota(jnp.int32, (SSM_GROUP, LANES), 1) < SSM_STATE
    x1 = jnp.where(low, bbr, bbi)
    x2 = jnp.where(low, -bbi, bbr)
    row = lax.broadcasted_iota(jnp.int32, (SSM_CW, LANES), 0)
    pr, pi, lr_t, li_t = _powers_by_bits(SSM_T - 1 - row // SSM_GROUP, lr, li, n_bits)
    w = pr * jnp.concatenate([x1] * SSM_T, axis=0) + pi * jnp.concatenate([x2] * SSM_T, axis=0)
    w_ref[:, :SSM_SW] = w.astype(BF16)
    w_ref[:, SSM_SW:] = pltpu.roll(w, SSM_STATE, axis=1).astype(BF16)
    da_ref[...] = lr_t
    db_ref[...] = jnp.where(low[:1], -li_t, li_t)

    lrs, lis, _, _ = _zoh(are_s_ref[...], aim_s_ref[...], dt)
    tau = lax.broadcasted_iota(jnp.int32, (SSM_STATE, SSM_CW), 1) // SSM_GROUP
    p0r, p0i, _, _ = _powers_by_bits(tau, lrs, lis, n_bits)
    cr, ci = cre_ref[...], cim_ref[...]
    gr, gi = _cmul(cr, ci, p0r, p0i)
    lhs = jnp.where(low, bbr, -bbi)
    k_all = jnp.dot(lhs, jnp.concatenate([gr, gi], axis=0),
                    preferred_element_type=F32, precision=lax.Precision.HIGHEST)
    crow = lax.broadcasted_iota(jnp.int32, (SSM_GROUP, SSM_CW), 0)
    clane = lax.broadcasted_iota(jnp.int32, (SSM_GROUP, SSM_CW), 1)
    k_all = k_all + jnp.where(crow == clane, d_ref[...], 0.0)
    for s in range(SSM_T):
        shifted = k_all if s == 0 else pltpu.roll(k_all, s * SSM_GROUP, axis=1)
        m_ref[s * SSM_GROUP:(s + 1) * SSM_GROUP, :] = jnp.where(
            clane >= s * SSM_GROUP, shifted, 0.0).astype(BF16)
    p1r, p1i = _cmul(p0r, p0i, lrs, lis)
    vr, vi = _cmul(cr, ci, p1r, p1i)
    v_ref[:SSM_STATE, :] = vr.astype(BF16)
    v_ref[SSM_STATE:, :] = (-vi).astype(BF16)


def _s5_prep(a_re, a_im, log_dt, b_re, b_im, c_re, c_im, d_skip):
    g = SSM_GROUPS
    dup = lambda a: jnp.concatenate([a, a], axis=-1)
    args = (
        log_dt.reshape(g, 1, 1),
        dup(a_re)[:, None, :], dup(a_im)[:, None, :],
        a_re[:, :, None], a_im[:, :, None],
        dup(b_re.transpose(0, 2, 1)), dup(b_im.transpose(0, 2, 1)),
        jnp.tile(c_re.transpose(0, 2, 1), (1, 1, SSM_T)),
        jnp.tile(c_im.transpose(0, 2, 1), (1, 1, SSM_T)),
        d_skip.reshape(g, SSM_GROUP, 1),
    )

    def spec(a):
        return pl.BlockSpec((None,) + a.shape[1:], lambda i: (i, 0, 0))

    out_shapes = [
        jax.ShapeDtypeStruct((g, SSM_CW, 2 * SSM_SW), BF16),
        jax.ShapeDtypeStruct((g, SSM_CW, SSM_CW), BF16),
        jax.ShapeDtypeStruct((g, SSM_SW, SSM_CW), BF16),
        jax.ShapeDtypeStruct((g, 1, SSM_SW), F32),
        jax.ShapeDtypeStruct((g, 1, SSM_SW), F32),
    ]
    wmat, mmat, vmat, dec_a, dec_b = pl.pallas_call(
        _s5_prep_kernel,
        grid=(g,),
        in_specs=[spec(a) for a in args],
        out_specs=[spec(s) for s in out_shapes],
        out_shape=out_shapes,
        compiler_params=_params(("parallel",)),
        name="s5_prep",
    )(*args)
    return wmat, mmat, vmat, dec_a.reshape(g, SSM_SW), dec_b.reshape(g, SSM_SW)


def _block_transpose(vs, lane_block):
    vs = list(vs)
    d = SSM_GB // 2
    while d:
        hi = (lane_block & d) != 0
        nxt = list(vs)
        for i in range(SSM_GB):
            if i & d:
                continue
            a, b = vs[i], vs[i + d]
            nxt[i] = jnp.where(hi, pltpu.roll(b, d * SSM_GROUP, axis=1), a)
            nxt[i + d] = jnp.where(hi, b, pltpu.roll(a, LANES - d * SSM_GROUP, axis=1))
        vs = nxt
        d //= 2
    return vs


S5_IN_ROWS = 16
S5_OUT_ROWS = SUBLANES
S5_UNROLL_IN = 2
S5_UNROLL_OUT = 4


def _s5_kernel(u_ref, w_ref, m_ref, v_ref, da_ref, db_ref, y_ref,
               uc_ref, wx_ref, ws_ref, xp_ref, yc_ref, *, n_chunks):
    halves = SSM_T // SSM_GB
    lane_block = lax.broadcasted_iota(jnp.int32, (1, LANES), 1) // SSM_GROUP

    def relayout_in(i, carry):
        for half in range(halves):
            vs = [u_ref[pl.ds(i * (S5_IN_ROWS * SSM_T) + half * SSM_GB + t, S5_IN_ROWS,
                              stride=SSM_T), :] for t in range(SSM_GB)]
            vs = _block_transpose(vs, lane_block)
            for g in range(SSM_GB):
                uc_ref[g, pl.ds(pl.multiple_of(i * S5_IN_ROWS, S5_IN_ROWS), S5_IN_ROWS),
                       half * LANES:(half + 1) * LANES] = vs[g].astype(BF16)
        return carry

    lax.fori_loop(0, n_chunks // S5_IN_ROWS, relayout_in, 0, unroll=S5_UNROLL_IN)

    for g in range(SSM_GB):
        w = jnp.dot(uc_ref[g], w_ref[g], preferred_element_type=F32)
        wx_ref[pl.ds(g, n_chunks, stride=SSM_GB), :] = w[:, :SSM_SW]
        ws_ref[pl.ds(g, n_chunks, stride=SSM_GB), :] = w[:, SSM_SW:]

    dec_a = da_ref[...]
    dec_b = db_ref[...]

    def step(n, carry):
        x, xs = carry
        rows = pl.ds(pl.multiple_of(n * SSM_GB, SSM_GB), SSM_GB)
        xp_ref[rows, :] = x
        return (dec_a * x + dec_b * xs + wx_ref[rows, :],
                dec_a * xs - dec_b * x + ws_ref[rows, :])

    zero = jnp.zeros((SSM_GB, SSM_SW), F32)
    lax.fori_loop(0, n_chunks, step, (zero, zero), unroll=8)

    for g in range(SSM_GB):
        y = jnp.dot(uc_ref[g], m_ref[g], preferred_element_type=F32)
        xp = xp_ref[pl.ds(g, n_chunks, stride=SSM_GB), :].astype(BF16)
        yc_ref[g] = y + jnp.dot(xp, v_ref[g], preferred_element_type=F32)

    def relayout_out(i, carry):
        for half in range(halves):
            vs = [yc_ref[g, pl.ds(pl.multiple_of(i * S5_OUT_ROWS, S5_OUT_ROWS), S5_OUT_ROWS),
                         half * LANES:(half + 1) * LANES] for g in range(SSM_GB)]
            vs = _block_transpose(vs, lane_block)
            for t in range(SSM_GB):
                y_ref[pl.ds(i * (S5_OUT_ROWS * SSM_T) + half * SSM_GB + t, S5_OUT_ROWS,
                            stride=SSM_T), :] = vs[t]
        return carry

    lax.fori_loop(0, n_chunks // S5_OUT_ROWS, relayout_out, 0, unroll=S5_UNROLL_OUT)


def _s5(u_slabs, mats, batch, seq):
    wmat, mmat, vmat, dec_a, dec_b = mats
    m = u_slabs.shape[1]
    n_chunks = seq // SSM_T

    def grp(shape):
        return pl.BlockSpec((SSM_GB,) + shape, lambda k, b: (k,) + (0,) * len(shape))

    seq_spec = pl.BlockSpec((None, seq, LANES), lambda k, b: (k, b, 0))
    return pl.pallas_call(
        functools.partial(_s5_kernel, n_chunks=n_chunks),
        grid=(SSM_SLABS, batch),
        in_specs=[
            seq_spec,
            grp((SSM_CW, 2 * SSM_SW)),
            grp((SSM_CW, SSM_CW)),
            grp((SSM_SW, SSM_CW)),
            grp((SSM_SW,)),
            grp((SSM_SW,)),
        ],
        out_specs=seq_spec,
        out_shape=jax.ShapeDtypeStruct((SSM_SLABS, m, LANES), F32),
        scratch_shapes=[
            pltpu.VMEM((SSM_GB, n_chunks, SSM_CW), BF16),
            pltpu.VMEM((SSM_GB * n_chunks, SSM_SW), F32),
            pltpu.VMEM((SSM_GB * n_chunks, SSM_SW), F32),
            pltpu.VMEM((SSM_GB * n_chunks, SSM_SW), F32),
            pltpu.VMEM((SSM_GB, n_chunks, SSM_CW), F32),
        ],
        compiler_params=_params(("parallel", "parallel")),
        name="s5_chunks",
    )(u_slabs, wmat, mmat, vmat, dec_a, dec_b)


def _outproj_kernel(x_ref, yr_ref, ys_ref, wglu_ref, bglu_ref, og_ref, wout_ref, o_ref):
    y1 = jax.nn.gelu(jnp.concatenate([ys_ref[k] for k in range(SSM_SLABS)], axis=-1))
    z = jnp.dot(y1.astype(BF16), wglu_ref[...], preferred_element_type=F32) + bglu_ref[...]
    y2 = y1 * jax.nn.sigmoid(z)
    y_ssm = _rms(y2, og_ref[...]).astype(BF16)
    acc = jnp.dot(yr_ref[...], wout_ref[:RET_WIDTH, :], preferred_element_type=F32)
    acc = acc + jnp.dot(y_ssm, wout_ref[RET_WIDTH:, :], preferred_element_type=F32)
    o_ref[...] = x_ref[...] + acc


def _outproj(x2d, y_ret, y_s5, w_glu, b_glu, out_g, w_out):
    m = x2d.shape[0]

    def rows(width):
        return pl.BlockSpec((ROW_TILE, width), lambda i: (i, 0))

    def whole(shape):
        return pl.BlockSpec(shape, lambda i: (0, 0))

    return pl.pallas_call(
        _outproj_kernel,
        grid=(m // ROW_TILE,),
        in_specs=[
            rows(D_MODEL), rows(RET_WIDTH),
            pl.BlockSpec((SSM_SLABS, ROW_TILE, LANES), lambda i: (0, i, 0)),
            whole((SSM_WIDTH, SSM_WIDTH)), whole((1, SSM_WIDTH)), whole((1, SSM_WIDTH)),
            whole((D_MODEL, D_MODEL)),
        ],
        out_specs=rows(D_MODEL),
        out_shape=jax.ShapeDtypeStruct((m, D_MODEL), F32),
        compiler_params=_params(("parallel",)),
        name="outproj",
    )(x2d, y_ret, y_s5, w_glu, b_glu, out_g, w_out)


def _ffn_kernel(x_ref, g_ref, wg_ref, wu_ref, wd_ref, gf_ref, o_ref, h_ref):
    j = pl.program_id(1)

    @pl.when(j == 0)
    def _():
        x = x_ref[...]
        h_ref[...] = _rms(x, g_ref[...]).astype(BF16)
        o_ref[...] = x

    h = h_ref[...]
    gate = jnp.dot(h, wg_ref[...], preferred_element_type=F32)
    up = jnp.dot(h, wu_ref[...], preferred_element_type=F32)
    act = (jax.nn.silu(gate) * up).astype(BF16)
    o_ref[...] += jnp.dot(act, wd_ref[...], preferred_element_type=F32)

    @pl.when(j == pl.num_programs(1) - 1)
    def _():
        o_ref[...] = _rms(o_ref[...], gf_ref[...])


def _ffn(x2d, gain, w_gate, w_up, w_down, gain_final):
    m = x2d.shape[0]
    return pl.pallas_call(
        _ffn_kernel,
        grid=(m // FFN_ROW_TILE, D_FF // FF_TILE),
        in_specs=[
            pl.BlockSpec((FFN_ROW_TILE, D_MODEL), lambda i, j: (i, 0)),
            pl.BlockSpec((1, D_MODEL), lambda i, j: (0, 0)),
            pl.BlockSpec((D_MODEL, FF_TILE), lambda i, j: (0, j)),
            pl.BlockSpec((D_MODEL, FF_TILE), lambda i, j: (0, j)),
            pl.BlockSpec((FF_TILE, D_MODEL), lambda i, j: (j, 0)),
            pl.BlockSpec((1, D_MODEL), lambda i, j: (0, 0)),
        ],
        out_specs=pl.BlockSpec((FFN_ROW_TILE, D_MODEL), lambda i, j: (i, 0)),
        out_shape=jax.ShapeDtypeStruct((m, D_MODEL), F32),
        scratch_shapes=[pltpu.VMEM((FFN_ROW_TILE, D_MODEL), BF16)],
        compiler_params=_params(("parallel", "arbitrary")),
        name="ffn",
    )(x2d, gain, w_gate, w_up, w_down, gain_final)


def _rope_tables(seq):
    half = RET_HEAD_DIM // 2
    pos = np.arange(seq, dtype=np.float64)
    freqs = ROPE_BASE ** (-np.arange(half, dtype=np.float64) / half)
    ang = pos[:, None] * freqs[None, :]
    cos = np.concatenate([np.cos(ang), np.cos(ang)], axis=-1)
    sin = np.concatenate([-np.sin(ang), np.sin(ang)], axis=-1)
    scale = np.array([1.0, RET_HEAD_DIM ** -0.5])[:, None, None]
    return (jnp.asarray(cos[None] * scale, dtype=F32), jnp.asarray(sin[None] * scale, dtype=F32))


def kernel(x, norm_mix_g, w_in, ret_gn_g, ssm_a_re, ssm_a_im, ssm_log_dt, ssm_b_re, ssm_b_im,
           ssm_c_re, ssm_c_im, ssm_d, ssm_w_glu, ssm_b_glu, ssm_out_g, w_out, norm_ffn_g,
           w_gate, w_up, w_down, norm_final_g):
    batch, seq, d = x.shape
    depth = w_in.shape[0]
    assert d == D_MODEL and seq % ROW_TILE == 0 and seq % RET_BLOCK == 0 and seq % SSM_T == 0
    m = batch * seq
    rope = _rope_tables(seq)
    x2d = x.reshape(m, d)
    ret_cols = RET_WIDTH // IN_COL_TILE

    for l in range(depth):
        gain = norm_mix_g[l][None]
        qk = _inproj(x2d, gain, w_in[l], 0, 2 * ret_cols, "rope", rope=rope, seq=seq)
        vg = _inproj(x2d, gain, w_in[l], 2 * ret_cols, 2 * ret_cols, "plain")
        u_slabs = _inproj(x2d, gain, w_in[l], 4 * ret_cols, 1, "slabs")
        y_ret = _retention(qk, vg, ret_gn_g[l][None], batch, seq)
        mats = _s5_prep(ssm_a_re[l], ssm_a_im[l], ssm_log_dt[l], ssm_b_re[l], ssm_b_im[l],
                        ssm_c_re[l], ssm_c_im[l], ssm_d[l])
        y_s5 = _s5(u_slabs, mats, batch, seq)

        x2d = _outproj(x2d, y_ret, y_s5, ssm_w_glu[l].astype(BF16), ssm_b_glu[l][None],
                       ssm_out_g[l][None], w_out[l].astype(BF16))
        last = l == depth - 1
        assert last, "fused final norm assumes a single layer"
        x2d = _ffn(x2d, norm_ffn_g[l][None], w_gate[l].astype(BF16), w_up[l].astype(BF16),
                   w_down[l].astype(BF16), norm_final_g[None])
    return x2d.reshape(batch, seq, d)
```

```python
import functools
import math

import jax
import jax.numpy as jnp
import numpy as np
from jax import lax
from jax.experimental import pallas as pl
from jax.experimental.pallas import tpu as pltpu

D_MODEL = 2048
CHUNK = 64
RET_WIDTH = D_MODEL // 2
RET_HEADS = 8
RET_HEAD_DIM = RET_WIDTH // RET_HEADS
SSM_WIDTH = D_MODEL - RET_WIDTH
SSM_GROUP = 16
SSM_GROUPS = SSM_WIDTH // SSM_GROUP
SSM_STATE = 64
D_FF = -(-8 * D_MODEL // (3 * 256)) * 256
IN_WIDTH = 4 * RET_WIDTH + SSM_WIDTH
ROPE_BASE = 10000.0
EPS = 1e-6

F32 = jnp.float32
BF16 = jnp.bfloat16

V7X_VMEM_BYTES = 64 * 1024 * 1024
VMEM_LIMIT = V7X_VMEM_BYTES - 8 * 1024 * 1024

ROW_TILE = 512
IN_COL_TILE = RET_WIDTH
FFN_ROW_TILE = 1024
FF_TILE = 512
RET_BLOCK = 256
SSM_T = 16
SSM_CW = SSM_T * SSM_GROUP
LANES = 128
SUBLANES = 8
SSM_GB = LANES // SSM_GROUP
SSM_SLABS = SSM_GROUPS // SSM_GB
SSM_SW = 2 * SSM_STATE


def _params(sem):
    return pltpu.CompilerParams(dimension_semantics=sem, vmem_limit_bytes=VMEM_LIMIT)


def _rms(x, g):
    return x * lax.rsqrt(jnp.mean(x * x, axis=-1, keepdims=True) + EPS) * g


def _inproj_kernel(x_ref, g_ref, w_ref, *rest, mode):
    if mode == "rope":
        cos_ref, sin_ref, o_ref, wb_ref = rest
    else:
        o_ref, wb_ref = rest

    @pl.when(pl.program_id(1) == 0)
    def _():
        wb_ref[...] = w_ref[...].astype(BF16)

    x = x_ref[...]
    inv = lax.rsqrt(jnp.mean(x * x, axis=-1, keepdims=True) + EPS)
    acc = jnp.dot((x * g_ref[...]).astype(BF16), wb_ref[...], preferred_element_type=F32)
    if mode == "rope":
        cos = cos_ref[...] * inv
        sin = sin_ref[...] * inv
        for h in range(RET_HEADS):
            hs = slice(h * RET_HEAD_DIM, (h + 1) * RET_HEAD_DIM)
            a = acc[:, hs]
            o_ref[:, hs] = (a * cos + pltpu.roll(a, RET_HEAD_DIM // 2, axis=1) * sin).astype(BF16)
    elif mode == "plain":
        o_ref[...] = (acc * inv).astype(BF16)
    else:
        acc = acc * inv
        for k in range(SSM_SLABS):
            o_ref[k] = acc[:, k * LANES:(k + 1) * LANES]


def _inproj(x2d, gain, w_in, col0, n_cols, mode, rope=None, seq=None):
    m = x2d.shape[0]
    in_specs = [
        pl.BlockSpec((ROW_TILE, D_MODEL), lambda j, i: (i, 0)),
        pl.BlockSpec((1, D_MODEL), lambda j, i: (0, 0)),
        pl.BlockSpec((D_MODEL, IN_COL_TILE), lambda j, i: (0, col0 + j)),
    ]
    args = [x2d, gain, w_in]
    if mode == "rope":
        tiles_per_seq = seq // ROW_TILE
        tab = pl.BlockSpec((None, ROW_TILE, RET_HEAD_DIM),
                           lambda j, i: (j, i % tiles_per_seq, 0))
        in_specs += [tab, tab]
        args += list(rope)
    if mode == "slabs":
        assert n_cols == 1
        out_spec = pl.BlockSpec((SSM_SLABS, ROW_TILE, LANES), lambda j, i: (0, i, 0))
        out_shape = jax.ShapeDtypeStruct((SSM_SLABS, m, LANES), F32)
    else:
        out_spec = pl.BlockSpec((ROW_TILE, IN_COL_TILE), lambda j, i: (i, j))
        out_shape = jax.ShapeDtypeStruct((m, n_cols * IN_COL_TILE), BF16)
    return pl.pallas_call(
        functools.partial(_inproj_kernel, mode=mode),
        grid=(n_cols, m // ROW_TILE),
        in_specs=in_specs,
        out_specs=out_spec,
        out_shape=out_shape,
        scratch_shapes=[pltpu.VMEM((D_MODEL, IN_COL_TILE), BF16)],
        compiler_params=_params(("arbitrary", "arbitrary")),
        name="inproj_" + mode,
    )(*args)


def _retention_kernel(q_ref, k_ref, v_ref, g_ref, dmat_ref, qdec_ref, kdec_ref,
                      gn_ref, o_ref, st_ref, *, block_decay):
    @pl.when(pl.program_id(1) == 0)
    def _():
        st_ref[...] = jnp.zeros_like(st_ref)

    for h in range(RET_HEADS):
        hs = slice(h * RET_HEAD_DIM, (h + 1) * RET_HEAD_DIM)
        q = q_ref[:, hs]
        k = k_ref[:, hs]
        v = v_ref[:, hs]
        s = lax.dot_general(q, k, (((1,), (1,)), ((), ())),
                            preferred_element_type=F32) * dmat_ref[h]
        out = jnp.dot(s.astype(BF16), v, preferred_element_type=F32)
        state = st_ref[h]
        qd = (q.astype(F32) * qdec_ref[h]).astype(BF16)
        out = out + jnp.dot(qd, state.astype(BF16), preferred_element_type=F32)
        kd = (k.astype(F32) * kdec_ref[h]).astype(BF16)
        kv = lax.dot_general(kd, v, (((0,), (0,)), ((), ())),
                             preferred_element_type=F32)
        st_ref[h] = state * block_decay[h] + kv
        mu = jnp.mean(out, axis=-1, keepdims=True)
        cen = out - mu
        var = jnp.mean(cen * cen, axis=-1, keepdims=True)
        normed = cen * lax.rsqrt(var + EPS) * gn_ref[:, hs]
        gate = g_ref[:, hs].astype(F32)
        o_ref[:, hs] = (jax.nn.silu(gate) * normed).astype(BF16)


def _retention_log_decay():
    return np.log1p(-(2.0 ** (-5.0 - np.arange(RET_HEADS, dtype=np.float64))))


def _retention_tables():
    log_g = _retention_log_decay()
    idx = np.arange(RET_BLOCK, dtype=np.float64)
    chunk = np.arange(RET_BLOCK) // CHUNK
    diff = idx[:, None] - idx[None, :]
    same = chunk[:, None] == chunk[None, :]
    earlier = chunk[None, :] < chunk[:, None]
    dist = np.where(same, np.abs(diff), diff)
    dmat = np.where((same | earlier)[None], np.exp(log_g[:, None, None] * dist[None]), 0.0)
    qdec = np.exp(log_g[:, None] * (idx + 1.0)[None, :])
    kdec = np.exp(log_g[:, None] * (RET_BLOCK - 1.0 - idx)[None, :])
    qdec = np.broadcast_to(qdec[:, :, None], (RET_HEADS, RET_BLOCK, RET_HEAD_DIM))
    kdec = np.broadcast_to(kdec[:, :, None], (RET_HEADS, RET_BLOCK, RET_HEAD_DIM))
    return tuple(jnp.asarray(a, dtype=F32) for a in (dmat, qdec, kdec))


def _retention(qk, vg, gn_gain, batch, seq):
    m = qk.shape[0]
    nblk = seq // RET_BLOCK
    dmat, qdec, kdec = _retention_tables()
    block_decay = tuple(float(v) for v in np.exp(_retention_log_decay() * RET_BLOCK))

    def col(c):
        return pl.BlockSpec((RET_BLOCK, RET_WIDTH), lambda b, t, c=c: (b * nblk + t, c))

    def whole(shape):
        return pl.BlockSpec(shape, lambda b, t: (0,) * len(shape))

    return pl.pallas_call(
        functools.partial(_retention_kernel, block_decay=block_decay),
        grid=(batch, nblk),
        in_specs=[
            col(0), col(1), col(0), col(1),
            whole((RET_HEADS, RET_BLOCK, RET_BLOCK)),
            whole((RET_HEADS, RET_BLOCK, RET_HEAD_DIM)),
            whole((RET_HEADS, RET_BLOCK, RET_HEAD_DIM)),
            whole((1, RET_WIDTH)),
        ],
        out_specs=pl.BlockSpec((RET_BLOCK, RET_WIDTH), lambda b, t: (b * nblk + t, 0)),
        out_shape=jax.ShapeDtypeStruct((m, RET_WIDTH), BF16),
        scratch_shapes=[pltpu.VMEM((RET_HEADS, RET_HEAD_DIM, RET_HEAD_DIM), F32)],
        compiler_params=_params(("parallel", "arbitrary")),
        name="retention",
    )(qk, qk, vg, vg, dmat, qdec, kdec, gn_gain)


def _cmul(ar, ai, br, bi):
    return ar * br - ai * bi, ar * bi + ai * br


def _zoh(a_re, a_im, dt):
    e = jnp.exp(a_re * dt)
    lr = e * jnp.cos(a_im * dt)
    li = e * jnp.sin(a_im * dt)
    inv = 1.0 / (a_re * a_re + a_im * a_im)
    xr = lr - 1.0
    return lr, li, (xr * a_re + li * a_im) * inv, (li * a_re - xr * a_im) * inv


def _powers_by_bits(expo, lr, li, n_bits):
    pr = jnp.ones(expo.shape, F32)
    pi = jnp.zeros(expo.shape, F32)
    qr, qi = lr, li
    for bit in range(n_bits):
        sel = (expo & (1 << bit)) != 0
        mr, mi = _cmul(pr, pi, qr, qi)
        pr = jnp.where(sel, mr, pr)
        pi = jnp.where(sel, mi, pi)
        qr, qi = _cmul(qr, qi, qr, qi)
    return pr, pi, qr, qi


def _s5_prep_kernel(*refs):
    for g in range(SSM_GB):
        _s5_prep_group(*[r.at[g] for r in refs])


def _s5_prep_group(logdt_ref, are_l_ref, aim_l_ref, are_s_ref, aim_s_ref, bre_ref, bim_ref,
                   cre_ref, cim_ref, d_ref, w_ref, m_ref, v_ref, da_ref, db_ref):
    n_bits = SSM_T.bit_length() - 1
    dt = jnp.exp(logdt_ref[...])

    lr, li, br, bi = _zoh(are_l_ref[...], aim_l_ref[...], dt)
    bbr, bbi = _cmul(br, bi, bre_ref[...], bim_ref[...])
    low = lax.broadcasted_iota(jnp.int32, (SSM_GROUP, LANES), 1) < SSM_STATE
    x1 = jnp.where(low, bbr, bbi)
    x2 = jnp.where(low, -bbi, bbr)
    row = lax.broadcasted_iota(jnp.int32, (SSM_CW, LANES), 0)
    pr, pi, lr_t, li_t = _powers_by_bits(SSM_T - 1 - row // SSM_GROUP, lr, li, n_bits)
    w = pr * jnp.concatenate([x1] * SSM_T, axis=0) + pi * jnp.concatenate([x2] * SSM_T, axis=0)
    w_ref[:, :SSM_SW] = w.astype(BF16)
    w_ref[:, SSM_SW:] = pltpu.roll(w, SSM_STATE, axis=1).astype(BF16)
    da_ref[...] = lr_t
    db_ref[...] = jnp.where(low[:1], -li_t, li_t)

    lrs, lis, _, _ = _zoh(are_s_ref[...], aim_s_ref[...], dt)
    tau = lax.broadcasted_iota(jnp.int32, (SSM_STATE, SSM_CW), 1) // SSM_GROUP
    p0r, p0i, _, _ = _powers_by_bits(tau, lrs, lis, n_bits)
    cr, ci = cre_ref[...], cim_ref[...]
    gr, gi = _cmul(cr, ci, p0r, p0i)
    lhs = jnp.where(low, bbr, -bbi)
    k_all = jnp.dot(lhs, jnp.concatenate([gr, gi], axis=0),
                    preferred_element_type=F32, precision=lax.Precision.HIGHEST)
    crow = lax.broadcasted_iota(jnp.int32, (SSM_GROUP, SSM_CW), 0)
    clane = lax.broadcasted_iota(jnp.int32, (SSM_GROUP, SSM_CW), 1)
    k_all = k_all + jnp.where(crow == clane, d_ref[...], 0.0)
    for s in range(SSM_T):
        shifted = k_all if s == 0 else pltpu.roll(k_all, s * SSM_GROUP, axis=1)
        m_ref[s * SSM_GROUP:(s + 1) * SSM_GROUP, :] = jnp.where(
            clane >= s * SSM_GROUP, shifted, 0.0).astype(BF16)
    p1r, p1i = _cmul(p0r, p0i, lrs, lis)
    vr, vi = _cmul(cr, ci, p1r, p1i)
    v_ref[:SSM_STATE, :] = vr.astype(BF16)
    v_ref[SSM_STATE:, :] = (-vi).astype(BF16)


def _s5_prep(a_re, a_im, log_dt, b_re, b_im, c_re, c_im, d_skip):
    g = SSM_GROUPS
    dup = lambda a: jnp.concatenate([a, a], axis=-1)
    args = (
        log_dt.reshape(g, 1, 1),
        dup(a_re)[:, None, :], dup(a_im)[:, None, :],
        a_re[:, :, None], a_im[:, :, None],
        dup(b_re.transpose(0, 2, 1)), dup(b_im.transpose(0, 2, 1)),
        jnp.tile(c_re.transpose(0, 2, 1), (1, 1, SSM_T)),
        jnp.tile(c_im.transpose(0, 2, 1), (1, 1, SSM_T)),
        d_skip.reshape(g, SSM_GROUP, 1),
    )

    def spec(a):
        return pl.BlockSpec((SSM_GB,) + a.shape[1:], lambda i: (i, 0, 0))

    out_shapes = [
        jax.ShapeDtypeStruct((g, SSM_CW, 2 * SSM_SW), BF16),
        jax.ShapeDtypeStruct((g, SSM_CW, SSM_CW), BF16),
        jax.ShapeDtypeStruct((g, SSM_SW, SSM_CW), BF16),
        jax.ShapeDtypeStruct((g, 1, SSM_SW), F32),
        jax.ShapeDtypeStruct((g, 1, SSM_SW), F32),
    ]
    wmat, mmat, vmat, dec_a, dec_b = pl.pallas_call(
        _s5_prep_kernel,
        grid=(g // SSM_GB,),
        in_specs=[spec(a) for a in args],
        out_specs=[spec(s) for s in out_shapes],
        out_shape=out_shapes,
        compiler_params=_params(("parallel",)),
        name="s5_prep",
    )(*args)
    return wmat, mmat, vmat, dec_a.reshape(g, SSM_SW), dec_b.reshape(g, SSM_SW)


def _chunk_permutation():
    n = SSM_GB * LANES
    t, g, c = np.meshgrid(np.arange(SSM_GB), np.arange(SSM_GB), np.arange(SSM_GROUP),
                          indexing="ij")
    p = np.zeros((n, n), np.float32)
    p[(t * LANES + g * SSM_GROUP + c).ravel(), (g * LANES + t * SSM_GROUP + c).ravel()] = 1.0
    return jnp.asarray(p, dtype=BF16)


def _s5_kernel(u_ref, p_ref, w_ref, m_ref, v_ref, da_ref, db_ref, y_ref,
               cat_ref, uc_ref, wx_ref, ws_ref, xp_ref, *, n_chunks):
    halves = SSM_T // SSM_GB
    half_w = SSM_GB * LANES

    for t in range(SSM_T):
        cat_ref[:, t * LANES:(t + 1) * LANES] = (
            u_ref[pl.ds(t, n_chunks, stride=SSM_T), :].astype(BF16))
    for half in range(halves):
        z = jnp.dot(cat_ref[:, half * half_w:(half + 1) * half_w], p_ref[...],
                    preferred_element_type=F32).astype(BF16)
        for g in range(SSM_GB):
            uc_ref[g, :, half * LANES:(half + 1) * LANES] = z[:, g * LANES:(g + 1) * LANES]

    for g in range(SSM_GB):
        w = jnp.dot(uc_ref[g], w_ref[g], preferred_element_type=F32)
        wx_ref[pl.ds(g, n_chunks, stride=SSM_GB), :] = w[:, :SSM_SW]
        ws_ref[pl.ds(g, n_chunks, stride=SSM_GB), :] = w[:, SSM_SW:]

    dec_a = da_ref[...]
    dec_b = db_ref[...]

    def step(n, carry):
        x, xs = carry
        rows = pl.ds(pl.multiple_of(n * SSM_GB, SSM_GB), SSM_GB)
        xp_ref[rows, :] = x
        return (dec_a * x + dec_b * xs + wx_ref[rows, :],
                dec_a * xs - dec_b * x + ws_ref[rows, :])

    zero = jnp.zeros((SSM_GB, SSM_SW), F32)
    lax.fori_loop(0, n_chunks, step, (zero, zero), unroll=8)

    for g in range(SSM_GB):
        y = jnp.dot(uc_ref[g], m_ref[g], preferred_element_type=F32)
        xp = xp_ref[pl.ds(g, n_chunks, stride=SSM_GB), :].astype(BF16)
        y = (y + jnp.dot(xp, v_ref[g], preferred_element_type=F32)).astype(BF16)
        for half in range(halves):
            cat_ref[:, half * half_w + g * LANES:half * half_w + (g + 1) * LANES] = (
                y[:, half * LANES:(half + 1) * LANES])

    for half in range(halves):
        z = jnp.dot(cat_ref[:, half * half_w:(half + 1) * half_w], p_ref[...],
                    preferred_element_type=F32)
        for t in range(SSM_GB):
            y_ref[pl.ds(half * SSM_GB + t, n_chunks, stride=SSM_T), :] = (
                z[:, t * LANES:(t + 1) * LANES])


def _s5(u_slabs, mats, batch, seq):
    wmat, mmat, vmat, dec_a, dec_b = mats
    m = u_slabs.shape[1]
    n_chunks = seq // SSM_T
    perm = _chunk_permutation()

    def grp(shape):
        return pl.BlockSpec((SSM_GB,) + shape, lambda k, b: (k,) + (0,) * len(shape))

    seq_spec = pl.BlockSpec((None, seq, LANES), lambda k, b: (k, b, 0))
    return pl.pallas_call(
        functools.partial(_s5_kernel, n_chunks=n_chunks),
        grid=(SSM_SLABS, batch),
        in_specs=[
            seq_spec,
            pl.BlockSpec(perm.shape, lambda k, b: (0, 0)),
            grp((SSM_CW, 2 * SSM_SW)),
            grp((SSM_CW, SSM_CW)),
            grp((SSM_SW, SSM_CW)),
            grp((SSM_SW,)),
            grp((SSM_SW,)),
        ],
        out_specs=seq_spec,
        out_shape=jax.ShapeDtypeStruct((SSM_SLABS, m, LANES), F32),
        scratch_shapes=[
            pltpu.VMEM((n_chunks, SSM_T * LANES), BF16),
            pltpu.VMEM((SSM_GB, n_chunks, SSM_CW), BF16),
            pltpu.VMEM((SSM_GB * n_chunks, SSM_SW), F32),
            pltpu.VMEM((SSM_GB * n_chunks, SSM_SW), F32),
            pltpu.VMEM((SSM_GB * n_chunks, SSM_SW), F32),
        ],
        compiler_params=_params(("parallel", "parallel")),
        name="s5_chunks",
    )(u_slabs, perm, wmat, mmat, vmat, dec_a, dec_b)


def _outproj_kernel(x_ref, yr_ref, ys_ref, wglu_ref, bglu_ref, og_ref, wout_ref, o_ref):
    y1 = jax.nn.gelu(jnp.concatenate([ys_ref[k] for k in range(SSM_SLABS)], axis=-1))
    z = jnp.dot(y1.astype(BF16), wglu_ref[...], preferred_element_type=F32) + bglu_ref[...]
    y2 = y1 * jax.nn.sigmoid(z)
    y_ssm = _rms(y2, og_ref[...]).astype(BF16)
    acc = jnp.dot(yr_ref[...], wout_ref[:RET_WIDTH, :], preferred_element_type=F32)
    acc = acc + jnp.dot(y_ssm, wout_ref[RET_WIDTH:, :], preferred_element_type=F32)
    o_ref[...] = x_ref[...] + acc


def _outproj(x2d, y_ret, y_s5, w_glu, b_glu, out_g, w_out):
    m = x2d.shape[0]

    def rows(width):
        return pl.BlockSpec((ROW_TILE, width), lambda i: (i, 0))

    def whole(shape):
        return pl.BlockSpec(shape, lambda i: (0, 0))

    return pl.pallas_call(
        _outproj_kernel,
        grid=(m // ROW_TILE,),
        in_specs=[
            rows(D_MODEL), rows(RET_WIDTH),
            pl.BlockSpec((SSM_SLABS, ROW_TILE, LANES), lambda i: (0, i, 0)),
            whole((SSM_WIDTH, SSM_WIDTH)), whole((1, SSM_WIDTH)), whole((1, SSM_WIDTH)),
            whole((D_MODEL, D_MODEL)),
        ],
        out_specs=rows(D_MODEL),
        out_shape=jax.ShapeDtypeStruct((m, D_MODEL), F32),
        compiler_params=_params(("parallel",)),
        name="outproj",
    )(x2d, y_ret, y_s5, w_glu, b_glu, out_g, w_out)


def _ffn_kernel(x_ref, g_ref, wg_ref, wu_ref, wd_ref, gf_ref, o_ref, h_ref):
    j = pl.program_id(1)

    @pl.when(j == 0)
    def _():
        x = x_ref[...]
        h_ref[...] = _rms(x, g_ref[...]).astype(BF16)
        o_ref[...] = x

    h = h_ref[...]
    gate = jnp.dot(h, wg_ref[...], preferred_element_type=F32)
    up = jnp.dot(h, wu_ref[...], preferred_element_type=F32)
    act = (jax.nn.silu(gate) * up).astype(BF16)
    o_ref[...] += jnp.dot(act, wd_ref[...], preferred_element_type=F32)

    @pl.when(j == pl.num_programs(1) - 1)
    def _():
        o_ref[...] = _rms(o_ref[...], gf_ref[...])


def _ffn(x2d, gain, w_gate, w_up, w_down, gain_final):
    m = x2d.shape[0]
    return pl.pallas_call(
        _ffn_kernel,
        grid=(m // FFN_ROW_TILE, D_FF // FF_TILE),
        in_specs=[
            pl.BlockSpec((FFN_ROW_TILE, D_MODEL), lambda i, j: (i, 0)),
            pl.BlockSpec((1, D_MODEL), lambda i, j: (0, 0)),
            pl.BlockSpec((D_MODEL, FF_TILE), lambda i, j: (0, j)),
            pl.BlockSpec((D_MODEL, FF_TILE), lambda i, j: (0, j)),
            pl.BlockSpec((FF_TILE, D_MODEL), lambda i, j: (j, 0)),
            pl.BlockSpec((1, D_MODEL), lambda i, j: (0, 0)),
        ],
        out_specs=pl.BlockSpec((FFN_ROW_TILE, D_MODEL), lambda i, j: (i, 0)),
        out_shape=jax.ShapeDtypeStruct((m, D_MODEL), F32),
        scratch_shapes=[pltpu.VMEM((FFN_ROW_TILE, D_MODEL), BF16)],
        compiler_params=_params(("parallel", "arbitrary")),
        name="ffn",
    )(x2d, gain, w_gate, w_up, w_down, gain_final)


def _rope_tables(seq):
    half = RET_HEAD_DIM // 2
    pos = np.arange(seq, dtype=np.float64)
    freqs = ROPE_BASE ** (-np.arange(half, dtype=np.float64) / half)
    ang = pos[:, None] * freqs[None, :]
    cos = np.concatenate([np.cos(ang), np.cos(ang)], axis=-1)
    sin = np.concatenate([-np.sin(ang), np.sin(ang)], axis=-1)
    scale = np.array([1.0, RET_HEAD_DIM ** -0.5])[:, None, None]
    return (jnp.asarray(cos[None] * scale, dtype=F32), jnp.asarray(sin[None] * scale, dtype=F32))


def kernel(x, norm_mix_g, w_in, ret_gn_g, ssm_a_re, ssm_a_im, ssm_log_dt, ssm_b_re, ssm_b_im,
           ssm_c_re, ssm_c_im, ssm_d, ssm_w_glu, ssm_b_glu, ssm_out_g, w_out, norm_ffn_g,
           w_gate, w_up, w_down, norm_final_g):
    batch, seq, d = x.shape
    depth = w_in.shape[0]
    assert d == D_MODEL and seq % ROW_TILE == 0 and seq % RET_BLOCK == 0 and seq % SSM_T == 0
    m = batch * seq
    rope = _rope_tables(seq)
    x2d = x.reshape(m, d)
    ret_cols = RET_WIDTH // IN_COL_TILE

    for l in range(depth):
        gain = norm_mix_g[l][None]
        qk = _inproj(x2d, gain, w_in[l], 0, 2 * ret_cols, "rope", rope=rope, seq=seq)
        vg = _inproj(x2d, gain, w_in[l], 2 * ret_cols, 2 * ret_cols, "plain")
        u_slabs = _inproj(x2d, gain, w_in[l], 4 * ret_cols, 1, "slabs")
        y_ret = _retention(qk, vg, ret_gn_g[l][None], batch, seq)
        mats = _s5_prep(ssm_a_re[l], ssm_a_im[l], ssm_log_dt[l], ssm_b_re[l], ssm_b_im[l],
                        ssm_c_re[l], ssm_c_im[l], ssm_d[l])
        y_s5 = _s5(u_slabs, mats, batch, seq)

        x2d = _outproj(x2d, y_ret, y_s5, ssm_w_glu[l].astype(BF16), ssm_b_glu[l][None],
                       ssm_out_g[l][None], w_out[l].astype(BF16))
        last = l == depth - 1
        assert last, "fused final norm assumes a single layer"
        x2d = _ffn(x2d, norm_ffn_g[l][None], w_gate[l].astype(BF16), w_up[l].astype(BF16),
                   w_down[l].astype(BF16), norm_final_g[None])
    return x2d.reshape(batch, seq, d)
```

```python
import functools
import math

import jax
import jax.numpy as jnp
import numpy as np
from jax import lax
from jax.experimental import pallas as pl
from jax.experimental.pallas import tpu as pltpu

D_MODEL = 2048
CHUNK = 64
RET_WIDTH = D_MODEL // 2
RET_HEADS = 8
RET_HEAD_DIM = RET_WIDTH // RET_HEADS
SSM_WIDTH = D_MODEL - RET_WIDTH
SSM_GROUP = 16
SSM_GROUPS = SSM_WIDTH // SSM_GROUP
SSM_STATE = 64
D_FF = -(-8 * D_MODEL // (3 * 256)) * 256
IN_WIDTH = 4 * RET_WIDTH + SSM_WIDTH
ROPE_BASE = 10000.0
EPS = 1e-6

F32 = jnp.float32
BF16 = jnp.bfloat16

V7X_VMEM_BYTES = 64 * 1024 * 1024
VMEM_LIMIT = V7X_VMEM_BYTES - 8 * 1024 * 1024

ROW_TILE = 512
IN_COL_TILE = RET_WIDTH
FFN_ROW_TILE = 1024
FF_TILE = 512
RET_BLOCK = 256
SSM_T = 16
SSM_CW = SSM_T * SSM_GROUP
LANES = 128
SUBLANES = 8
BF16_ROWS = 16
SSM_GB = LANES // SSM_GROUP
SSM_SLABS = SSM_GROUPS // SSM_GB
SSM_SW = 2 * SSM_STATE


def _params(sem):
    return pltpu.CompilerParams(dimension_semantics=sem, vmem_limit_bytes=VMEM_LIMIT)


def _rms(x, g):
    return x * lax.rsqrt(jnp.mean(x * x, axis=-1, keepdims=True) + EPS) * g


def _inproj_kernel(x_ref, g_ref, w_ref, *rest, mode, n_cast):
    n_in = 2 if mode == "rope" else 0
    cast_in = rest[n_in:n_in + n_cast]
    o_ref = rest[n_in + n_cast]
    cast_out = rest[n_in + n_cast + 1:n_in + 2 * n_cast + 1]
    wb_ref = rest[-1]
    if mode == "rope":
        cos_ref, sin_ref = rest[:2]

    for src, dst in zip(cast_in, cast_out):
        dst[...] = src[...].astype(BF16)

    @pl.when(pl.program_id(1) == 0)
    def _():
        wb_ref[...] = w_ref[...].astype(BF16)

    x = x_ref[...]
    inv = lax.rsqrt(jnp.mean(x * x, axis=-1, keepdims=True) + EPS)
    acc = jnp.dot((x * g_ref[...]).astype(BF16), wb_ref[...], preferred_element_type=F32)
    if mode == "rope":
        cos = cos_ref[...] * inv
        sin = sin_ref[...] * inv
        for h in range(RET_HEADS):
            hs = slice(h * RET_HEAD_DIM, (h + 1) * RET_HEAD_DIM)
            a = acc[:, hs]
            o_ref[:, hs] = (a * cos + pltpu.roll(a, RET_HEAD_DIM // 2, axis=1) * sin).astype(BF16)
    elif mode == "plain":
        o_ref[...] = (acc * inv).astype(BF16)
    else:
        acc = acc * inv
        for k in range(SSM_SLABS):
            o_ref[k] = acc[:, k * LANES:(k + 1) * LANES]


def _inproj(x2d, gain, w_in, col0, n_cols, mode, rope=None, seq=None, cast=()):
    m = x2d.shape[0]
    n_rows = m // ROW_TILE
    n_steps = n_cols * n_rows
    in_specs = [
        pl.BlockSpec((ROW_TILE, D_MODEL), lambda j, i: (i, 0)),
        pl.BlockSpec((1, D_MODEL), lambda j, i: (0, 0)),
        pl.BlockSpec((D_MODEL, IN_COL_TILE), lambda j, i: (0, col0 + j)),
    ]
    args = [x2d, gain, w_in]
    if mode == "rope":
        tiles_per_seq = seq // ROW_TILE
        tab = pl.BlockSpec((None, ROW_TILE, RET_HEAD_DIM),
                           lambda j, i: (j, i % tiles_per_seq, 0))
        in_specs += [tab, tab]
        args += list(rope)
    if mode == "slabs":
        assert n_cols == 1
        out_spec = pl.BlockSpec((SSM_SLABS, ROW_TILE, LANES), lambda j, i: (0, i, 0))
        out_shape = jax.ShapeDtypeStruct((SSM_SLABS, m, LANES), F32)
    else:
        out_spec = pl.BlockSpec((ROW_TILE, IN_COL_TILE), lambda j, i: (i, j))
        out_shape = jax.ShapeDtypeStruct((m, n_cols * IN_COL_TILE), BF16)
    cast_specs = []
    for w in cast:
        rows, cols = w.shape
        assert rows % (n_steps * BF16_ROWS) == 0, (w.shape, n_steps)
        cast_specs.append(pl.BlockSpec((rows // n_steps, cols), lambda j, i: (j * n_rows + i, 0)))
    outs = pl.pallas_call(
        functools.partial(_inproj_kernel, mode=mode, n_cast=len(cast)),
        grid=(n_cols, n_rows),
        in_specs=in_specs + cast_specs,
        out_specs=[out_spec] + cast_specs,
        out_shape=[out_shape] + [jax.ShapeDtypeStruct(w.shape, BF16) for w in cast],
        scratch_shapes=[pltpu.VMEM((D_MODEL, IN_COL_TILE), BF16)],
        compiler_params=_params(("arbitrary", "arbitrary")),
        name="inproj_" + mode,
    )(*args, *cast)
    return outs[0], outs[1:]


def _retention_kernel(q_ref, k_ref, v_ref, g_ref, dmat_ref, qdec_ref, kdec_ref,
                      gn_ref, o_ref, st_ref, *, block_decay):
    @pl.when(pl.program_id(1) == 0)
    def _():
        st_ref[...] = jnp.zeros_like(st_ref)

    for h in range(RET_HEADS):
        hs = slice(h * RET_HEAD_DIM, (h + 1) * RET_HEAD_DIM)
        q = q_ref[:, hs]
        k = k_ref[:, hs]
        v = v_ref[:, hs]
        s = lax.dot_general(q, k, (((1,), (1,)), ((), ())),
                            preferred_element_type=F32) * dmat_ref[h]
        out = jnp.dot(s.astype(BF16), v, preferred_element_type=F32)
        state = st_ref[h]
        qd = (q.astype(F32) * qdec_ref[h]).astype(BF16)
        out = out + jnp.dot(qd, state.astype(BF16), preferred_element_type=F32)
        kd = (k.astype(F32) * kdec_ref[h]).astype(BF16)
        kv = lax.dot_general(kd, v, (((0,), (0,)), ((), ())),
                             preferred_element_type=F32)
        st_ref[h] = state * block_decay[h] + kv
        mu = jnp.mean(out, axis=-1, keepdims=True)
        cen = out - mu
        var = jnp.mean(cen * cen, axis=-1, keepdims=True)
        normed = cen * lax.rsqrt(var + EPS) * gn_ref[:, hs]
        gate = g_ref[:, hs].astype(F32)
        o_ref[:, hs] = (jax.nn.silu(gate) * normed).astype(BF16)


def _retention_log_decay():
    return np.log1p(-(2.0 ** (-5.0 - np.arange(RET_HEADS, dtype=np.float64))))


def _retention_tables():
    log_g = _retention_log_decay()
    idx = np.arange(RET_BLOCK, dtype=np.float64)
    chunk = np.arange(RET_BLOCK) // CHUNK
    diff = idx[:, None] - idx[None, :]
    same = chunk[:, None] == chunk[None, :]
    earlier = chunk[None, :] < chunk[:, None]
    dist = np.where(same, np.abs(diff), diff)
    dmat = np.where((same | earlier)[None], np.exp(log_g[:, None, None] * dist[None]), 0.0)
    qdec = np.exp(log_g[:, None] * (idx + 1.0)[None, :])
    kdec = np.exp(log_g[:, None] * (RET_BLOCK - 1.0 - idx)[None, :])
    qdec = np.broadcast_to(qdec[:, :, None], (RET_HEADS, RET_BLOCK, RET_HEAD_DIM))
    kdec = np.broadcast_to(kdec[:, :, None], (RET_HEADS, RET_BLOCK, RET_HEAD_DIM))
    return tuple(jnp.asarray(a, dtype=F32) for a in (dmat, qdec, kdec))


def _retention(qk, vg, gn_gain, batch, seq):
    m = qk.shape[0]
    nblk = seq // RET_BLOCK
    dmat, qdec, kdec = _retention_tables()
    block_decay = tuple(float(v) for v in np.exp(_retention_log_decay() * RET_BLOCK))

    def col(c):
        return pl.BlockSpec((RET_BLOCK, RET_WIDTH), lambda b, t, c=c: (b * nblk + t, c))

    def whole(shape):
        return pl.BlockSpec(shape, lambda b, t: (0,) * len(shape))

    return pl.pallas_call(
        functools.partial(_retention_kernel, block_decay=block_decay),
        grid=(batch, nblk),
        in_specs=[
            col(0), col(1), col(0), col(1),
            whole((RET_HEADS, RET_BLOCK, RET_BLOCK)),
            whole((RET_HEADS, RET_BLOCK, RET_HEAD_DIM)),
            whole((RET_HEADS, RET_BLOCK, RET_HEAD_DIM)),
            whole((1, RET_WIDTH)),
        ],
        out_specs=pl.BlockSpec((RET_BLOCK, RET_WIDTH), lambda b, t: (b * nblk + t, 0)),
        out_shape=jax.ShapeDtypeStruct((m, RET_WIDTH), BF16),
        scratch_shapes=[pltpu.VMEM((RET_HEADS, RET_HEAD_DIM, RET_HEAD_DIM), F32)],
        compiler_params=_params(("parallel", "arbitrary")),
        name="retention",
    )(qk, qk, vg, vg, dmat, qdec, kdec, gn_gain)


def _cmul(ar, ai, br, bi):
    return ar * br - ai * bi, ar * bi + ai * br


def _zoh(a_re, a_im, dt):
    e = jnp.exp(a_re * dt)
    lr = e * jnp.cos(a_im * dt)
    li = e * jnp.sin(a_im * dt)
    inv = 1.0 / (a_re * a_re + a_im * a_im)
    xr = lr - 1.0
    return lr, li, (xr * a_re + li * a_im) * inv, (li * a_re - xr * a_im) * inv


def _powers_by_bits(expo, lr, li, n_bits):
    pr = jnp.ones(expo.shape, F32)
    pi = jnp.zeros(expo.shape, F32)
    qr, qi = lr, li
    for bit in range(n_bits):
        sel = (expo & (1 << bit)) != 0
        mr, mi = _cmul(pr, pi, qr, qi)
        pr = jnp.where(sel, mr, pr)
        pi = jnp.where(sel, mi, pi)
        qr, qi = _cmul(qr, qi, qr, qi)
    return pr, pi, qr, qi


def _s5_prep_kernel(*refs):
    for g in range(SSM_GB):
        _s5_prep_group(*[r.at[g] for r in refs])


def _s5_prep_group(logdt_ref, are_l_ref, aim_l_ref, are_s_ref, aim_s_ref, bre_ref, bim_ref,
                   cre_ref, cim_ref, d_ref, w_ref, m_ref, v_ref, da_ref, db_ref):
    n_bits = SSM_T.bit_length() - 1
    dt = jnp.exp(logdt_ref[...])

    lr, li, br, bi = _zoh(are_l_ref[...], aim_l_ref[...], dt)
    bbr, bbi = _cmul(br, bi, bre_ref[...], bim_ref[...])
    low = lax.broadcasted_iota(jnp.int32, (SSM_GROUP, LANES), 1) < SSM_STATE
    x1 = jnp.where(low, bbr, bbi)
    x2 = jnp.where(low, -bbi, bbr)
    row = lax.broadcasted_iota(jnp.int32, (SSM_CW, LANES), 0)
    pr, pi, lr_t, li_t = _powers_by_bits(SSM_T - 1 - row // SSM_GROUP, lr, li, n_bits)
    w = pr * jnp.concatenate([x1] * SSM_T, axis=0) + pi * jnp.concatenate([x2] * SSM_T, axis=0)
    w_ref[:, :SSM_SW] = w.astype(BF16)
    w_ref[:, SSM_SW:] = pltpu.roll(w, SSM_STATE, axis=1).astype(BF16)
    da_ref[...] = lr_t
    db_ref[...] = jnp.where(low[:1], -li_t, li_t)

    lrs, lis, _, _ = _zoh(are_s_ref[...], aim_s_ref[...], dt)
    tau = lax.broadcasted_iota(jnp.int32, (SSM_STATE, SSM_CW), 1) // SSM_GROUP
    p0r, p0i, _, _ = _powers_by_bits(tau, lrs, lis, n_bits)
    cr, ci = cre_ref[...], cim_ref[...]
    gr, gi = _cmul(cr, ci, p0r, p0i)
    lhs = jnp.where(low, bbr, -bbi)
    k_all = jnp.dot(lhs, jnp.concatenate([gr, gi], axis=0),
                    preferred_element_type=F32, precision=lax.Precision.HIGHEST)
    crow = lax.broadcasted_iota(jnp.int32, (SSM_GROUP, SSM_CW), 0)
    clane = lax.broadcasted_iota(jnp.int32, (SSM_GROUP, SSM_CW), 1)
    k_all = k_all + jnp.where(crow == clane, d_ref[...], 0.0)
    for s in range(SSM_T):
        shifted = k_all if s == 0 else pltpu.roll(k_all, s * SSM_GROUP, axis=1)
        m_ref[s * SSM_GROUP:(s + 1) * SSM_GROUP, :] = jnp.where(
            clane >= s * SSM_GROUP, shifted, 0.0).astype(BF16)
    p1r, p1i = _cmul(p0r, p0i, lrs, lis)
    vr, vi = _cmul(cr, ci, p1r, p1i)
    v_ref[:SSM_STATE, :] = vr.astype(BF16)
    v_ref[SSM_STATE:, :] = (-vi).astype(BF16)


def _s5_prep(a_re, a_im, log_dt, b_re, b_im, c_re, c_im, d_skip):
    g = SSM_GROUPS
    dup = lambda a: jnp.concatenate([a, a], axis=-1)
    args = (
        log_dt.reshape(g, 1, 1),
        dup(a_re)[:, None, :], dup(a_im)[:, None, :],
        a_re[:, :, None], a_im[:, :, None],
        dup(b_re.transpose(0, 2, 1)), dup(b_im.transpose(0, 2, 1)),
        jnp.tile(c_re.transpose(0, 2, 1), (1, 1, SSM_T)),
        jnp.tile(c_im.transpose(0, 2, 1), (1, 1, SSM_T)),
        d_skip.reshape(g, SSM_GROUP, 1),
    )

    def spec(a):
        return pl.BlockSpec((SSM_GB,) + a.shape[1:], lambda i: (i, 0, 0))

    out_shapes = [
        jax.ShapeDtypeStruct((g, SSM_CW, 2 * SSM_SW), BF16),
        jax.ShapeDtypeStruct((g, SSM_CW, SSM_CW), BF16),
        jax.ShapeDtypeStruct((g, SSM_SW, SSM_CW), BF16),
        jax.ShapeDtypeStruct((g, 1, SSM_SW), F32),
        jax.ShapeDtypeStruct((g, 1, SSM_SW), F32),
    ]
    wmat, mmat, vmat, dec_a, dec_b = pl.pallas_call(
        _s5_prep_kernel,
        grid=(g // SSM_GB,),
        in_specs=[spec(a) for a in args],
        out_specs=[spec(s) for s in out_shapes],
        out_shape=out_shapes,
        compiler_params=_params(("parallel",)),
        name="s5_prep",
    )(*args)
    return wmat, mmat, vmat, dec_a.reshape(g, SSM_SW), dec_b.reshape(g, SSM_SW)


def _chunk_permutation():
    n = SSM_GB * LANES
    t, g, c = np.meshgrid(np.arange(SSM_GB), np.arange(SSM_GB), np.arange(SSM_GROUP),
                          indexing="ij")
    p = np.zeros((n, n), np.float32)
    p[(t * LANES + g * SSM_GROUP + c).ravel(), (g * LANES + t * SSM_GROUP + c).ravel()] = 1.0
    return jnp.asarray(p, dtype=BF16)


def _s5_kernel(u_ref, p_ref, w_ref, m_ref, v_ref, da_ref, db_ref, y_ref,
               cat_ref, uc_ref, wx_ref, ws_ref, xp_ref, *, n_chunks):
    halves = SSM_T // SSM_GB
    half_w = SSM_GB * LANES

    for t in range(SSM_T):
        cat_ref[:, t * LANES:(t + 1) * LANES] = (
            u_ref[pl.ds(t, n_chunks, stride=SSM_T), :].astype(BF16))
    for half in range(halves):
        z = jnp.dot(cat_ref[:, half * half_w:(half + 1) * half_w], p_ref[...],
                    preferred_element_type=F32).astype(BF16)
        for g in range(SSM_GB):
            uc_ref[g, :, half * LANES:(half + 1) * LANES] = z[:, g * LANES:(g + 1) * LANES]

    for g in range(SSM_GB):
        w = jnp.dot(uc_ref[g], w_ref[g], preferred_element_type=F32)
        wx_ref[pl.ds(g, n_chunks, stride=SSM_GB), :] = w[:, :SSM_SW]
        ws_ref[pl.ds(g, n_chunks, stride=SSM_GB), :] = w[:, SSM_SW:]

    dec_a = da_ref[...]
    dec_b = db_ref[...]

    def step(n, carry):
        x, xs = carry
        rows = pl.ds(pl.multiple_of(n * SSM_GB, SSM_GB), SSM_GB)
        xp_ref[rows, :] = x
        return (dec_a * x + dec_b * xs + wx_ref[rows, :],
                dec_a * xs - dec_b * x + ws_ref[rows, :])

    zero = jnp.zeros((SSM_GB, SSM_SW), F32)
    lax.fori_loop(0, n_chunks, step, (zero, zero), unroll=8)

    for g in range(SSM_GB):
        y = jnp.dot(uc_ref[g], m_ref[g], preferred_element_type=F32)
        xp = xp_ref[pl.ds(g, n_chunks, stride=SSM_GB), :].astype(BF16)
        y = (y + jnp.dot(xp, v_ref[g], preferred_element_type=F32)).astype(BF16)
        for half in range(halves):
            cat_ref[:, half * half_w + g * LANES:half * half_w + (g + 1) * LANES] = (
                y[:, half * LANES:(half + 1) * LANES])

    for half in range(halves):
        z = jnp.dot(cat_ref[:, half * half_w:(half + 1) * half_w], p_ref[...],
                    preferred_element_type=F32)
        for t in range(SSM_GB):
            y_ref[pl.ds(half * SSM_GB + t, n_chunks, stride=SSM_T), :] = (
                z[:, t * LANES:(t + 1) * LANES])


def _s5(u_slabs, mats, batch, seq):
    wmat, mmat, vmat, dec_a, dec_b = mats
    m = u_slabs.shape[1]
    n_chunks = seq // SSM_T
    perm = _chunk_permutation()

    def grp(shape):
        return pl.BlockSpec((SSM_GB,) + shape, lambda k, b: (k,) + (0,) * len(shape))

    seq_spec = pl.BlockSpec((None, seq, LANES), lambda k, b: (k, b, 0))
    return pl.pallas_call(
        functools.partial(_s5_kernel, n_chunks=n_chunks),
        grid=(SSM_SLABS, batch),
        in_specs=[
            seq_spec,
            pl.BlockSpec(perm.shape, lambda k, b: (0, 0)),
            grp((SSM_CW, 2 * SSM_SW)),
            grp((SSM_CW, SSM_CW)),
            grp((SSM_SW, SSM_CW)),
            grp((SSM_SW,)),
            grp((SSM_SW,)),
        ],
        out_specs=seq_spec,
        out_shape=jax.ShapeDtypeStruct((SSM_SLABS, m, LANES), F32),
        scratch_shapes=[
            pltpu.VMEM((n_chunks, SSM_T * LANES), BF16),
            pltpu.VMEM((SSM_GB, n_chunks, SSM_CW), BF16),
            pltpu.VMEM((SSM_GB * n_chunks, SSM_SW), F32),
            pltpu.VMEM((SSM_GB * n_chunks, SSM_SW), F32),
            pltpu.VMEM((SSM_GB * n_chunks, SSM_SW), F32),
        ],
        compiler_params=_params(("parallel", "parallel")),
        name="s5_chunks",
    )(u_slabs, perm, wmat, mmat, vmat, dec_a, dec_b)


def _outproj_kernel(x_ref, yr_ref, ys_ref, wglu_ref, bglu_ref, og_ref, wout_ref, o_ref):
    y1 = jax.nn.gelu(jnp.concatenate([ys_ref[k] for k in range(SSM_SLABS)], axis=-1))
    z = jnp.dot(y1.astype(BF16), wglu_ref[...], preferred_element_type=F32) + bglu_ref[...]
    y2 = y1 * jax.nn.sigmoid(z)
    y_ssm = _rms(y2, og_ref[...]).astype(BF16)
    acc = jnp.dot(yr_ref[...], wout_ref[:RET_WIDTH, :], preferred_element_type=F32)
    acc = acc + jnp.dot(y_ssm, wout_ref[RET_WIDTH:, :], preferred_element_type=F32)
    o_ref[...] = x_ref[...] + acc


def _outproj(x2d, y_ret, y_s5, w_glu, b_glu, out_g, w_out):
    m = x2d.shape[0]

    def rows(width):
        return pl.BlockSpec((ROW_TILE, width), lambda i: (i, 0))

    def whole(shape):
        return pl.BlockSpec(shape, lambda i: (0, 0))

    return pl.pallas_call(
        _outproj_kernel,
        grid=(m // ROW_TILE,),
        in_specs=[
            rows(D_MODEL), rows(RET_WIDTH),
            pl.BlockSpec((SSM_SLABS, ROW_TILE, LANES), lambda i: (0, i, 0)),
            whole((SSM_WIDTH, SSM_WIDTH)), whole((1, SSM_WIDTH)), whole((1, SSM_WIDTH)),
            whole((D_MODEL, D_MODEL)),
        ],
        out_specs=rows(D_MODEL),
        out_shape=jax.ShapeDtypeStruct((m, D_MODEL), F32),
        compiler_params=_params(("parallel",)),
        name="outproj",
    )(x2d, y_ret, y_s5, w_glu, b_glu, out_g, w_out)


def _ffn_kernel(x_ref, g_ref, wg_ref, wu_ref, wd_ref, gf_ref, o_ref, h_ref):
    j = pl.program_id(1)
    last = pl.num_programs(1) - 1

    def step(first, final):
        if first:
            x = x_ref[...]
            h = _rms(x, g_ref[...]).astype(BF16)
            h_ref[...] = h
            base = x
        else:
            h = h_ref[...]
            base = o_ref[...]
        gate = jnp.dot(h, wg_ref[...], preferred_element_type=F32)
        up = jnp.dot(h, wu_ref[...], preferred_element_type=F32)
        act = (jax.nn.silu(gate) * up).astype(BF16)
        o = base + jnp.dot(act, wd_ref[...], preferred_element_type=F32)
        o_ref[...] = _rms(o, gf_ref[...]) if final else o

    pl.when(j == 0)(lambda: step(True, False))
    pl.when((j > 0) & (j < last))(lambda: step(False, False))
    pl.when(j == last)(lambda: step(False, True))


def _ffn(x2d, gain, w_gate, w_up, w_down, gain_final):
    m = x2d.shape[0]
    return pl.pallas_call(
        _ffn_kernel,
        grid=(m // FFN_ROW_TILE, D_FF // FF_TILE),
        in_specs=[
            pl.BlockSpec((FFN_ROW_TILE, D_MODEL), lambda i, j: (i, 0)),
            pl.BlockSpec((1, D_MODEL), lambda i, j: (0, 0)),
            pl.BlockSpec((D_MODEL, FF_TILE), lambda i, j: (0, j)),
            pl.BlockSpec((D_MODEL, FF_TILE), lambda i, j: (0, j)),
            pl.BlockSpec((FF_TILE, D_MODEL), lambda i, j: (j, 0)),
            pl.BlockSpec((1, D_MODEL), lambda i, j: (0, 0)),
        ],
        out_specs=pl.BlockSpec((FFN_ROW_TILE, D_MODEL), lambda i, j: (i, 0)),
        out_shape=jax.ShapeDtypeStruct((m, D_MODEL), F32),
        scratch_shapes=[pltpu.VMEM((FFN_ROW_TILE, D_MODEL), BF16)],
        compiler_params=_params(("parallel", "arbitrary")),
        name="ffn",
    )(x2d, gain, w_gate, w_up, w_down, gain_final)


def _rope_tables(seq):
    half = RET_HEAD_DIM // 2
    pos = np.arange(seq, dtype=np.float64)
    freqs = ROPE_BASE ** (-np.arange(half, dtype=np.float64) / half)
    ang = pos[:, None] * freqs[None, :]
    cos = np.concatenate([np.cos(ang), np.cos(ang)], axis=-1)
    sin = np.concatenate([-np.sin(ang), np.sin(ang)], axis=-1)
    scale = np.array([1.0, RET_HEAD_DIM ** -0.5])[:, None, None]
    return (jnp.asarray(cos[None] * scale, dtype=F32), jnp.asarray(sin[None] * scale, dtype=F32))


def kernel(x, norm_mix_g, w_in, ret_gn_g, ssm_a_re, ssm_a_im, ssm_log_dt, ssm_b_re, ssm_b_im,
           ssm_c_re, ssm_c_im, ssm_d, ssm_w_glu, ssm_b_glu, ssm_out_g, w_out, norm_ffn_g,
           w_gate, w_up, w_down, norm_final_g):
    batch, seq, d = x.shape
    depth = w_in.shape[0]
    assert d == D_MODEL and seq % ROW_TILE == 0 and seq % RET_BLOCK == 0 and seq % SSM_T == 0
    m = batch * seq
    rope = _rope_tables(seq)
    x2d = x.reshape(m, d)
    ret_cols = RET_WIDTH // IN_COL_TILE

    for l in range(depth):
        gain = norm_mix_g[l][None]
        qk, (w_gate_b, w_glu_b) = _inproj(x2d, gain, w_in[l], 0, 2 * ret_cols, "rope", rope=rope,
                                          seq=seq, cast=(w_gate[l], ssm_w_glu[l]))
        vg, (w_up_b, w_out_b) = _inproj(x2d, gain, w_in[l], 2 * ret_cols, 2 * ret_cols, "plain",
                                        cast=(w_up[l], w_out[l]))
        u_slabs, (w_down_b,) = _inproj(x2d, gain, w_in[l], 4 * ret_cols, 1, "slabs",
                                       cast=(w_down[l],))
        y_ret = _retention(qk, vg, ret_gn_g[l][None], batch, seq)
        mats = _s5_prep(ssm_a_re[l], ssm_a_im[l], ssm_log_dt[l], ssm_b_re[l], ssm_b_im[l],
                        ssm_c_re[l], ssm_c_im[l], ssm_d[l])
        y_s5 = _s5(u_slabs, mats, batch, seq)

        x2d = _outproj(x2d, y_ret, y_s5, w_glu_b, ssm_b_glu[l][None], ssm_out_g[l][None], w_out_b)
        last = l == depth - 1
        assert last, "fused final norm assumes a single layer"
        x2d = _ffn(x2d, norm_ffn_g[l][None], w_gate_b, w_up_b, w_down_b, norm_final_g[None])
    return x2d.reshape(batch, seq, d)
```

```python
import functools
import math

import jax
import jax.numpy as jnp
import numpy as np
from jax import lax
from jax.experimental import pallas as pl
from jax.experimental.pallas import tpu as pltpu

D_MODEL = 2048
CHUNK = 64
RET_WIDTH = D_MODEL // 2
RET_HEADS = 8
RET_HEAD_DIM = RET_WIDTH // RET_HEADS
SSM_WIDTH = D_MODEL - RET_WIDTH
SSM_GROUP = 16
SSM_GROUPS = SSM_WIDTH // SSM_GROUP
SSM_STATE = 64
D_FF = -(-8 * D_MODEL // (3 * 256)) * 256
IN_WIDTH = 4 * RET_WIDTH + SSM_WIDTH
ROPE_BASE = 10000.0
EPS = 1e-6

F32 = jnp.float32
BF16 = jnp.bfloat16

V7X_VMEM_BYTES = 64 * 1024 * 1024
VMEM_LIMIT = V7X_VMEM_BYTES - 8 * 1024 * 1024

ROW_TILE = 512
IN_COL_TILE = RET_WIDTH
FFN_ROW_TILE = 1024
FF_TILE = 512
RET_BLOCK = 256
SSM_T = 16
SSM_CW = SSM_T * SSM_GROUP
LANES = 128
SUBLANES = 8
BF16_ROWS = 16
SSM_GB = LANES // SSM_GROUP
SSM_SLABS = SSM_GROUPS // SSM_GB
SSM_SW = 2 * SSM_STATE


def _params(sem):
    return pltpu.CompilerParams(dimension_semantics=sem, vmem_limit_bytes=VMEM_LIMIT)


def _rms(x, g):
    return x * lax.rsqrt(jnp.mean(x * x, axis=-1, keepdims=True) + EPS) * g


def _side_cast_specs(weights, n_steps, step_of):
    specs = []
    for w in weights:
        rows, cols = w.shape
        assert rows % (n_steps * BF16_ROWS) == 0, (w.shape, n_steps)
        specs.append(pl.BlockSpec((rows // n_steps, cols),
                                  lambda *idx, step_of=step_of: (step_of(*idx), 0)))
    return specs


def _side_cast(srcs, dsts):
    for src, dst in zip(srcs, dsts):
        dst[...] = src[...].astype(BF16)


def _inproj_kernel(x_ref, g_ref, w_ref, cos_ref, sin_ref, qk_ref, vg_ref, u_ref):
    x = x_ref[...]
    inv = lax.rsqrt(jnp.mean(x * x, axis=-1, keepdims=True) + EPS)
    h = (x * g_ref[...]).astype(BF16)

    def block(c):
        return jnp.dot(h, w_ref[:, c * IN_COL_TILE:(c + 1) * IN_COL_TILE],
                       preferred_element_type=F32)

    for c in range(2):
        acc = block(c)
        cos = cos_ref[c] * inv
        sin = sin_ref[c] * inv
        for h_i in range(RET_HEADS):
            lo = c * RET_WIDTH + h_i * RET_HEAD_DIM
            a = acc[:, h_i * RET_HEAD_DIM:(h_i + 1) * RET_HEAD_DIM]
            qk_ref[:, lo:lo + RET_HEAD_DIM] = (
                a * cos + pltpu.roll(a, RET_HEAD_DIM // 2, axis=1) * sin).astype(BF16)
    for c in range(2):
        vg_ref[:, c * RET_WIDTH:(c + 1) * RET_WIDTH] = (block(2 + c) * inv).astype(BF16)
    acc = block(4) * inv
    for k in range(SSM_SLABS):
        u_ref[k] = acc[:, k * LANES:(k + 1) * LANES]


def _inproj(x2d, gain, w_bf16, rope, seq):
    m = x2d.shape[0]
    assert IN_COL_TILE == RET_WIDTH == SSM_WIDTH
    tiles_per_seq = seq // ROW_TILE
    tab = pl.BlockSpec((2, ROW_TILE, RET_HEAD_DIM), lambda i: (0, i % tiles_per_seq, 0))
    row2 = pl.BlockSpec((ROW_TILE, 2 * RET_WIDTH), lambda i: (i, 0))
    return pl.pallas_call(
        _inproj_kernel,
        grid=(m // ROW_TILE,),
        in_specs=[
            pl.BlockSpec((ROW_TILE, D_MODEL), lambda i: (i, 0)),
            pl.BlockSpec((1, D_MODEL), lambda i: (0, 0)),
            pl.BlockSpec((D_MODEL, IN_WIDTH), lambda i: (0, 0), pipeline_mode=pl.Buffered(1)),
            tab, tab,
        ],
        out_specs=[row2, row2,
                   pl.BlockSpec((SSM_SLABS, ROW_TILE, LANES), lambda i: (0, i, 0))],
        out_shape=[jax.ShapeDtypeStruct((m, 2 * RET_WIDTH), BF16),
                   jax.ShapeDtypeStruct((m, 2 * RET_WIDTH), BF16),
                   jax.ShapeDtypeStruct((SSM_SLABS, m, LANES), F32)],
        compiler_params=_params(("parallel",)),
        name="inproj",
    )(x2d, gain, w_bf16, *rope)


def _retention_kernel(q_ref, k_ref, v_ref, g_ref, dmat_ref, qdec_ref, kdec_ref,
                      gn_ref, *rest, block_decay, n_cast):
    o_ref, st_ref = rest[n_cast], rest[-1]
    _side_cast(rest[:n_cast], rest[n_cast + 1:-1])

    @pl.when(pl.program_id(1) == 0)
    def _():
        st_ref[...] = jnp.zeros_like(st_ref)

    for h in range(RET_HEADS):
        hs = slice(h * RET_HEAD_DIM, (h + 1) * RET_HEAD_DIM)
        q = q_ref[:, hs]
        k = k_ref[:, hs]
        v = v_ref[:, hs]
        s = lax.dot_general(q, k, (((1,), (1,)), ((), ())),
                            preferred_element_type=F32) * dmat_ref[h]
        out = jnp.dot(s.astype(BF16), v, preferred_element_type=F32)
        state = st_ref[h]
        qd = (q.astype(F32) * qdec_ref[h]).astype(BF16)
        out = out + jnp.dot(qd, state.astype(BF16), preferred_element_type=F32)
        kd = (k.astype(F32) * kdec_ref[h]).astype(BF16)
        kv = lax.dot_general(kd, v, (((0,), (0,)), ((), ())),
                             preferred_element_type=F32)
        st_ref[h] = state * block_decay[h] + kv
        mu = jnp.mean(out, axis=-1, keepdims=True)
        cen = out - mu
        var = jnp.mean(cen * cen, axis=-1, keepdims=True)
        normed = cen * lax.rsqrt(var + EPS) * gn_ref[:, hs]
        gate = g_ref[:, hs].astype(F32)
        o_ref[:, hs] = (jax.nn.silu(gate) * normed).astype(BF16)


def _retention_log_decay():
    return np.log1p(-(2.0 ** (-5.0 - np.arange(RET_HEADS, dtype=np.float64))))


def _retention_tables():
    log_g = _retention_log_decay()
    idx = np.arange(RET_BLOCK, dtype=np.float64)
    chunk = np.arange(RET_BLOCK) // CHUNK
    diff = idx[:, None] - idx[None, :]
    same = chunk[:, None] == chunk[None, :]
    earlier = chunk[None, :] < chunk[:, None]
    dist = np.where(same, np.abs(diff), diff)
    dmat = np.where((same | earlier)[None], np.exp(log_g[:, None, None] * dist[None]), 0.0)
    qdec = np.exp(log_g[:, None] * (idx + 1.0)[None, :])
    kdec = np.exp(log_g[:, None] * (RET_BLOCK - 1.0 - idx)[None, :])
    qdec = np.broadcast_to(qdec[:, :, None], (RET_HEADS, RET_BLOCK, RET_HEAD_DIM))
    kdec = np.broadcast_to(kdec[:, :, None], (RET_HEADS, RET_BLOCK, RET_HEAD_DIM))
    return tuple(jnp.asarray(a, dtype=F32) for a in (dmat, qdec, kdec))


def _retention(qk, vg, gn_gain, batch, seq, cast=()):
    m = qk.shape[0]
    nblk = seq // RET_BLOCK
    cast_specs = _side_cast_specs(cast, batch * nblk, lambda b, t: b * nblk + t)
    dmat, qdec, kdec = _retention_tables()
    block_decay = tuple(float(v) for v in np.exp(_retention_log_decay() * RET_BLOCK))

    def col(c):
        return pl.BlockSpec((RET_BLOCK, RET_WIDTH), lambda b, t, c=c: (b * nblk + t, c))

    def whole(shape):
        return pl.BlockSpec(shape, lambda b, t: (0,) * len(shape))

    outs = pl.pallas_call(
        functools.partial(_retention_kernel, block_decay=block_decay, n_cast=len(cast)),
        grid=(batch, nblk),
        in_specs=[
            col(0), col(1), col(0), col(1),
            whole((RET_HEADS, RET_BLOCK, RET_BLOCK)),
            whole((RET_HEADS, RET_BLOCK, RET_HEAD_DIM)),
            whole((RET_HEADS, RET_BLOCK, RET_HEAD_DIM)),
            whole((1, RET_WIDTH)),
        ] + cast_specs,
        out_specs=[pl.BlockSpec((RET_BLOCK, RET_WIDTH), lambda b, t: (b * nblk + t, 0))]
        + cast_specs,
        out_shape=[jax.ShapeDtypeStruct((m, RET_WIDTH), BF16)]
        + [jax.ShapeDtypeStruct(w.shape, BF16) for w in cast],
        scratch_shapes=[pltpu.VMEM((RET_HEADS, RET_HEAD_DIM, RET_HEAD_DIM), F32)],
        compiler_params=_params(("parallel", "arbitrary")),
        name="retention",
    )(qk, qk, vg, vg, dmat, qdec, kdec, gn_gain, *cast)
    return outs[0], outs[1:]


def _cmul(ar, ai, br, bi):
    return ar * br - ai * bi, ar * bi + ai * br


def _zoh(a_re, a_im, dt):
    e = jnp.exp(a_re * dt)
    lr = e * jnp.cos(a_im * dt)
    li = e * jnp.sin(a_im * dt)
    inv = 1.0 / (a_re * a_re + a_im * a_im)
    xr = lr - 1.0
    return lr, li, (xr * a_re + li * a_im) * inv, (li * a_re - xr * a_im) * inv


def _powers_by_bits(expo, lr, li, n_bits):
    pr = jnp.ones(expo.shape, F32)
    pi = jnp.zeros(expo.shape, F32)
    qr, qi = lr, li
    for bit in range(n_bits):
        sel = (expo & (1 << bit)) != 0
        mr, mi = _cmul(pr, pi, qr, qi)
        pr = jnp.where(sel, mr, pr)
        pi = jnp.where(sel, mi, pi)
        qr, qi = _cmul(qr, qi, qr, qi)
    return pr, pi, qr, qi


S5_PREP_INPUTS = 10
S5_PREP_OUTPUTS = 5


def _s5_prep_kernel(*refs, n_cast):
    ins = refs[:S5_PREP_INPUTS]
    outs = refs[S5_PREP_INPUTS + n_cast:S5_PREP_INPUTS + n_cast + S5_PREP_OUTPUTS]
    _side_cast(refs[S5_PREP_INPUTS:S5_PREP_INPUTS + n_cast],
               refs[S5_PREP_INPUTS + n_cast + S5_PREP_OUTPUTS:])
    for g in range(SSM_GB):
        _s5_prep_group(*[r.at[g] for r in ins + outs])


def _s5_prep_group(logdt_ref, are_l_ref, aim_l_ref, are_s_ref, aim_s_ref, bre_ref, bim_ref,
                   cre_ref, cim_ref, d_ref, w_ref, m_ref, v_ref, da_ref, db_ref):
    n_bits = SSM_T.bit_length() - 1
    dt = jnp.exp(logdt_ref[...])

    lr, li, br, bi = _zoh(are_l_ref[...], aim_l_ref[...], dt)
    bbr, bbi = _cmul(br, bi, bre_ref[...], bim_ref[...])
    low = lax.broadcasted_iota(jnp.int32, (SSM_GROUP, LANES), 1) < SSM_STATE
    x1 = jnp.where(low, bbr, bbi)
    x2 = jnp.where(low, -bbi, bbr)
    row = lax.broadcasted_iota(jnp.int32, (SSM_CW, LANES), 0)
    pr, pi, lr_t, li_t = _powers_by_bits(SSM_T - 1 - row // SSM_GROUP, lr, li, n_bits)
    w = pr * jnp.concatenate([x1] * SSM_T, axis=0) + pi * jnp.concatenate([x2] * SSM_T, axis=0)
    w_ref[:, :SSM_SW] = w.astype(BF16)
    w_ref[:, SSM_SW:] = pltpu.roll(w, SSM_STATE, axis=1).astype(BF16)
    da_ref[...] = lr_t
    db_ref[...] = jnp.where(low[:1], -li_t, li_t)

    lrs, lis, _, _ = _zoh(are_s_ref[...], aim_s_ref[...], dt)
    tau = lax.broadcasted_iota(jnp.int32, (SSM_STATE, SSM_CW), 1) // SSM_GROUP
    p0r, p0i, _, _ = _powers_by_bits(tau, lrs, lis, n_bits)
    cr, ci = cre_ref[...], cim_ref[...]
    gr, gi = _cmul(cr, ci, p0r, p0i)
    lhs = jnp.where(low, bbr, -bbi)
    k_all = jnp.dot(lhs, jnp.concatenate([gr, gi], axis=0),
                    preferred_element_type=F32, precision=lax.Precision.HIGHEST)
    crow = lax.broadcasted_iota(jnp.int32, (SSM_GROUP, SSM_CW), 0)
    clane = lax.broadcasted_iota(jnp.int32, (SSM_GROUP, SSM_CW), 1)
    k_all = k_all + jnp.where(crow == clane, d_ref[...], 0.0)
    for s in range(SSM_T):
        shifted = k_all if s == 0 else pltpu.roll(k_all, s * SSM_GROUP, axis=1)
        m_ref[s * SSM_GROUP:(s + 1) * SSM_GROUP, :] = jnp.where(
            clane >= s * SSM_GROUP, shifted, 0.0).astype(BF16)
    p1r, p1i = _cmul(p0r, p0i, lrs, lis)
    vr, vi = _cmul(cr, ci, p1r, p1i)
    v_ref[:SSM_STATE, :] = vr.astype(BF16)
    v_ref[SSM_STATE:, :] = (-vi).astype(BF16)


def _s5_prep(a_re, a_im, log_dt, b_re, b_im, c_re, c_im, d_skip, cast=()):
    g = SSM_GROUPS
    cast_specs = _side_cast_specs(cast, g // SSM_GB, lambda i: i)
    dup = lambda a: jnp.concatenate([a, a], axis=-1)
    args = (
        log_dt.reshape(g, 1, 1),
        dup(a_re)[:, None, :], dup(a_im)[:, None, :],
        a_re[:, :, None], a_im[:, :, None],
        dup(b_re.transpose(0, 2, 1)), dup(b_im.transpose(0, 2, 1)),
        jnp.tile(c_re.transpose(0, 2, 1), (1, 1, SSM_T)),
        jnp.tile(c_im.transpose(0, 2, 1), (1, 1, SSM_T)),
        d_skip.reshape(g, SSM_GROUP, 1),
    )

    def spec(a):
        return pl.BlockSpec((SSM_GB,) + a.shape[1:], lambda i: (i, 0, 0))

    out_shapes = [
        jax.ShapeDtypeStruct((g, SSM_CW, 2 * SSM_SW), BF16),
        jax.ShapeDtypeStruct((g, SSM_CW, SSM_CW), BF16),
        jax.ShapeDtypeStruct((g, SSM_SW, SSM_CW), BF16),
        jax.ShapeDtypeStruct((g, 1, SSM_SW), F32),
        jax.ShapeDtypeStruct((g, 1, SSM_SW), F32),
    ]
    assert len(args) == S5_PREP_INPUTS and len(out_shapes) == S5_PREP_OUTPUTS
    outs = pl.pallas_call(
        functools.partial(_s5_prep_kernel, n_cast=len(cast)),
        grid=(g // SSM_GB,),
        in_specs=[spec(a) for a in args] + cast_specs,
        out_specs=[spec(s) for s in out_shapes] + cast_specs,
        out_shape=out_shapes + [jax.ShapeDtypeStruct(w.shape, BF16) for w in cast],
        compiler_params=_params(("parallel",)),
        name="s5_prep",
    )(*args, *cast)
    wmat, mmat, vmat, dec_a, dec_b = outs[:S5_PREP_OUTPUTS]
    mats = (wmat, mmat, vmat, dec_a.reshape(g, SSM_SW), dec_b.reshape(g, SSM_SW))
    return mats, outs[S5_PREP_OUTPUTS:]


def _chunk_permutation():
    n = SSM_GB * LANES
    t, g, c = np.meshgrid(np.arange(SSM_GB), np.arange(SSM_GB), np.arange(SSM_GROUP),
                          indexing="ij")
    p = np.zeros((n, n), np.float32)
    p[(t * LANES + g * SSM_GROUP + c).ravel(), (g * LANES + t * SSM_GROUP + c).ravel()] = 1.0
    return jnp.asarray(p, dtype=BF16)


def _s5_kernel(u_ref, p_ref, w_ref, m_ref, v_ref, da_ref, db_ref, *rest, n_chunks, n_cast):
    y_ref = rest[n_cast]
    cat_ref, uc_ref, wx_ref, ws_ref, xp_ref = rest[2 * n_cast + 1:]
    _side_cast(rest[:n_cast], rest[n_cast + 1:2 * n_cast + 1])
    halves = SSM_T // SSM_GB
    half_w = SSM_GB * LANES

    for t in range(SSM_T):
        cat_ref[:, t * LANES:(t + 1) * LANES] = (
            u_ref[pl.ds(t, n_chunks, stride=SSM_T), :].astype(BF16))
    for half in range(halves):
        z = jnp.dot(cat_ref[:, half * half_w:(half + 1) * half_w], p_ref[...],
                    preferred_element_type=F32).astype(BF16)
        for g in range(SSM_GB):
            uc_ref[g, :, half * LANES:(half + 1) * LANES] = z[:, g * LANES:(g + 1) * LANES]

    for g in range(SSM_GB):
        w = jnp.dot(uc_ref[g], w_ref[g], preferred_element_type=F32)
        wx_ref[pl.ds(g, n_chunks, stride=SSM_GB), :] = w[:, :SSM_SW]
        ws_ref[pl.ds(g, n_chunks, stride=SSM_GB), :] = w[:, SSM_SW:]

    dec_a = da_ref[...]
    dec_b = db_ref[...]

    def step(n, carry):
        x, xs = carry
        rows = pl.ds(pl.multiple_of(n * SSM_GB, SSM_GB), SSM_GB)
        xp_ref[rows, :] = x
        return (dec_a * x + dec_b * xs + wx_ref[rows, :],
                dec_a * xs - dec_b * x + ws_ref[rows, :])

    zero = jnp.zeros((SSM_GB, SSM_SW), F32)
    lax.fori_loop(0, n_chunks, step, (zero, zero), unroll=8)

    for g in range(SSM_GB):
        y = jnp.dot(uc_ref[g], m_ref[g], preferred_element_type=F32)
        xp = xp_ref[pl.ds(g, n_chunks, stride=SSM_GB), :].astype(BF16)
        y = (y + jnp.dot(xp, v_ref[g], preferred_element_type=F32)).astype(BF16)
        for half in range(halves):
            cat_ref[:, half * half_w + g * LANES:half * half_w + (g + 1) * LANES] = (
                y[:, half * LANES:(half + 1) * LANES])

    for half in range(halves):
        z = jnp.dot(cat_ref[:, half * half_w:(half + 1) * half_w], p_ref[...],
                    preferred_element_type=F32)
        for t in range(SSM_GB):
            y_ref[pl.ds(half * SSM_GB + t, n_chunks, stride=SSM_T), :] = (
                z[:, t * LANES:(t + 1) * LANES])


def _s5(u_slabs, mats, batch, seq, cast=()):
    wmat, mmat, vmat, dec_a, dec_b = mats
    m = u_slabs.shape[1]
    n_chunks = seq // SSM_T
    perm = _chunk_permutation()
    cast_specs = _side_cast_specs(cast, SSM_SLABS * batch, lambda k, b: k * batch + b)

    def grp(shape):
        return pl.BlockSpec((SSM_GB,) + shape, lambda k, b: (k,) + (0,) * len(shape))

    seq_spec = pl.BlockSpec((None, seq, LANES), lambda k, b: (k, b, 0))
    outs = pl.pallas_call(
        functools.partial(_s5_kernel, n_chunks=n_chunks, n_cast=len(cast)),
        grid=(SSM_SLABS, batch),
        in_specs=[
            seq_spec,
            pl.BlockSpec(perm.shape, lambda k, b: (0, 0)),
            grp((SSM_CW, 2 * SSM_SW)),
            grp((SSM_CW, SSM_CW)),
            grp((SSM_SW, SSM_CW)),
            grp((SSM_SW,)),
            grp((SSM_SW,)),
        ] + cast_specs,
        out_specs=[seq_spec] + cast_specs,
        out_shape=[jax.ShapeDtypeStruct((SSM_SLABS, m, LANES), F32)]
        + [jax.ShapeDtypeStruct(w.shape, BF16) for w in cast],
        scratch_shapes=[
            pltpu.VMEM((n_chunks, SSM_T * LANES), BF16),
            pltpu.VMEM((SSM_GB, n_chunks, SSM_CW), BF16),
            pltpu.VMEM((SSM_GB * n_chunks, SSM_SW), F32),
            pltpu.VMEM((SSM_GB * n_chunks, SSM_SW), F32),
            pltpu.VMEM((SSM_GB * n_chunks, SSM_SW), F32),
        ],
        compiler_params=_params(("parallel", "parallel")),
        name="s5_chunks",
    )(u_slabs, perm, wmat, mmat, vmat, dec_a, dec_b, *cast)
    return outs[0], outs[1:]


def _outproj_kernel(x_ref, yr_ref, ys_ref, wglu_ref, bglu_ref, og_ref, wout_ref, o_ref):
    y1 = jax.nn.gelu(jnp.concatenate([ys_ref[k] for k in range(SSM_SLABS)], axis=-1))
    z = jnp.dot(y1.astype(BF16), wglu_ref[...], preferred_element_type=F32) + bglu_ref[...]
    y2 = y1 * jax.nn.sigmoid(z)
    y_ssm = _rms(y2, og_ref[...]).astype(BF16)
    acc = jnp.dot(yr_ref[...], wout_ref[:RET_WIDTH, :], preferred_element_type=F32)
    acc = acc + jnp.dot(y_ssm, wout_ref[RET_WIDTH:, :], preferred_element_type=F32)
    o_ref[...] = x_ref[...] + acc


def _outproj(x2d, y_ret, y_s5, w_glu, b_glu, out_g, w_out):
    m = x2d.shape[0]

    def rows(width):
        return pl.BlockSpec((ROW_TILE, width), lambda i: (i, 0))

    def whole(shape):
        return pl.BlockSpec(shape, lambda i: (0, 0))

    return pl.pallas_call(
        _outproj_kernel,
        grid=(m // ROW_TILE,),
        in_specs=[
            rows(D_MODEL), rows(RET_WIDTH),
            pl.BlockSpec((SSM_SLABS, ROW_TILE, LANES), lambda i: (0, i, 0)),
            whole((SSM_WIDTH, SSM_WIDTH)), whole((1, SSM_WIDTH)), whole((1, SSM_WIDTH)),
            whole((D_MODEL, D_MODEL)),
        ],
        out_specs=rows(D_MODEL),
        out_shape=jax.ShapeDtypeStruct((m, D_MODEL), F32),
        compiler_params=_params(("parallel",)),
        name="outproj",
    )(x2d, y_ret, y_s5, w_glu, b_glu, out_g, w_out)


def _ffn_kernel(x_ref, g_ref, wg_ref, wu_ref, wd_ref, gf_ref, o_ref, h_ref):
    j = pl.program_id(1)
    last = pl.num_programs(1) - 1

    def step(first, final):
        if first:
            x = x_ref[...]
            h = _rms(x, g_ref[...]).astype(BF16)
            h_ref[...] = h
            base = x
        else:
            h = h_ref[...]
            base = o_ref[...]
        gate = jnp.dot(h, wg_ref[...], preferred_element_type=F32)
        up = jnp.dot(h, wu_ref[...], preferred_element_type=F32)
        act = (jax.nn.silu(gate) * up).astype(BF16)
        o = base + jnp.dot(act, wd_ref[...], preferred_element_type=F32)
        o_ref[...] = _rms(o, gf_ref[...]) if final else o

    pl.when(j == 0)(lambda: step(True, False))
    pl.when((j > 0) & (j < last))(lambda: step(False, False))
    pl.when(j == last)(lambda: step(False, True))


def _ffn(x2d, gain, w_gate, w_up, w_down, gain_final):
    m = x2d.shape[0]
    return pl.pallas_call(
        _ffn_kernel,
        grid=(m // FFN_ROW_TILE, D_FF // FF_TILE),
        in_specs=[
            pl.BlockSpec((FFN_ROW_TILE, D_MODEL), lambda i, j: (i, 0)),
            pl.BlockSpec((1, D_MODEL), lambda i, j: (0, 0)),
            pl.BlockSpec((D_MODEL, FF_TILE), lambda i, j: (0, j)),
            pl.BlockSpec((D_MODEL, FF_TILE), lambda i, j: (0, j)),
            pl.BlockSpec((FF_TILE, D_MODEL), lambda i, j: (j, 0)),
            pl.BlockSpec((1, D_MODEL), lambda i, j: (0, 0)),
        ],
        out_specs=pl.BlockSpec((FFN_ROW_TILE, D_MODEL), lambda i, j: (i, 0)),
        out_shape=jax.ShapeDtypeStruct((m, D_MODEL), F32),
        scratch_shapes=[pltpu.VMEM((FFN_ROW_TILE, D_MODEL), BF16)],
        compiler_params=_params(("parallel", "arbitrary")),
        name="ffn",
    )(x2d, gain, w_gate, w_up, w_down, gain_final)


def _rope_tables(seq):
    half = RET_HEAD_DIM // 2
    pos = np.arange(seq, dtype=np.float64)
    freqs = ROPE_BASE ** (-np.arange(half, dtype=np.float64) / half)
    ang = pos[:, None] * freqs[None, :]
    cos = np.concatenate([np.cos(ang), np.cos(ang)], axis=-1)
    sin = np.concatenate([-np.sin(ang), np.sin(ang)], axis=-1)
    scale = np.array([1.0, RET_HEAD_DIM ** -0.5])[:, None, None]
    return (jnp.asarray(cos[None] * scale, dtype=F32), jnp.asarray(sin[None] * scale, dtype=F32))


def kernel(x, norm_mix_g, w_in, ret_gn_g, ssm_a_re, ssm_a_im, ssm_log_dt, ssm_b_re, ssm_b_im,
           ssm_c_re, ssm_c_im, ssm_d, ssm_w_glu, ssm_b_glu, ssm_out_g, w_out, norm_ffn_g,
           w_gate, w_up, w_down, norm_final_g):
    batch, seq, d = x.shape
    depth = w_in.shape[0]
    assert d == D_MODEL and seq % ROW_TILE == 0 and seq % RET_BLOCK == 0 and seq % SSM_T == 0
    m = batch * seq
    rope = _rope_tables(seq)
    x2d = x.reshape(m, d)

    for l in range(depth):
        mats, (w_in_b,) = _s5_prep(ssm_a_re[l], ssm_a_im[l], ssm_log_dt[l], ssm_b_re[l],
                                   ssm_b_im[l], ssm_c_re[l], ssm_c_im[l], ssm_d[l],
                                   cast=(w_in[l],))
        qk, vg, u_slabs = _inproj(x2d, norm_mix_g[l][None], w_in_b, rope, seq)
        y_ret, (w_gate_b, w_up_b, w_out_b, w_glu_b) = _retention(
            qk, vg, ret_gn_g[l][None], batch, seq,
            cast=(w_gate[l], w_up[l], w_out[l], ssm_w_glu[l]))
        y_s5, (w_down_b,) = _s5(u_slabs, mats, batch, seq, cast=(w_down[l],))

        x2d = _outproj(x2d, y_ret, y_s5, w_glu_b, ssm_b_glu[l][None], ssm_out_g[l][None], w_out_b)
        last = l == depth - 1
        assert last, "fused final norm assumes a single layer"
        x2d = _ffn(x2d, norm_ffn_g[l][None], w_gate_b, w_up_b, w_down_b, norm_final_g[None])
    return x2d.reshape(batch, seq, d)
```

```python
import functools
import math

import jax
import jax.numpy as jnp
import numpy as np
from jax import lax
from jax.experimental import pallas as pl
from jax.experimental.pallas import tpu as pltpu

D_MODEL = 2048
CHUNK = 64
RET_WIDTH = D_MODEL // 2
RET_HEADS = 8
RET_HEAD_DIM = RET_WIDTH // RET_HEADS
SSM_WIDTH = D_MODEL - RET_WIDTH
SSM_GROUP = 16
SSM_GROUPS = SSM_WIDTH // SSM_GROUP
SSM_STATE = 64
D_FF = -(-8 * D_MODEL // (3 * 256)) * 256
IN_WIDTH = 4 * RET_WIDTH + SSM_WIDTH
ROPE_BASE = 10000.0
EPS = 1e-6

F32 = jnp.float32
BF16 = jnp.bfloat16

V7X_VMEM_BYTES = 64 * 1024 * 1024
VMEM_LIMIT = V7X_VMEM_BYTES - 8 * 1024 * 1024

ROW_TILE = 512
IN_COL_TILE = RET_WIDTH
FFN_ROW_TILE = 1024
FF_TILE = 512
RET_BLOCK = 256
SSM_T = 16
SSM_CW = SSM_T * SSM_GROUP
LANES = 128
SUBLANES = 8
BF16_ROWS = 16
SSM_GB = LANES // SSM_GROUP
SSM_SLABS = SSM_GROUPS // SSM_GB
SSM_SW = 2 * SSM_STATE


def _params(sem):
    return pltpu.CompilerParams(dimension_semantics=sem, vmem_limit_bytes=VMEM_LIMIT)


def _rms(x, g):
    return x * lax.rsqrt(jnp.mean(x * x, axis=-1, keepdims=True) + EPS) * g


def _side_cast_specs(weights, n_steps, step_of):
    specs = []
    for w in weights:
        rows, cols = w.shape
        assert rows % (n_steps * BF16_ROWS) == 0, (w.shape, n_steps)
        specs.append(pl.BlockSpec((rows // n_steps, cols),
                                  lambda *idx, step_of=step_of: (step_of(*idx), 0)))
    return specs


def _side_cast(srcs, dsts):
    for src, dst in zip(srcs, dsts):
        dst[...] = src[...].astype(BF16)


def _inproj_kernel(x_ref, g_ref, w_ref, cos_ref, sin_ref, *rest, n_cast):
    qk_ref, vg_ref, u_ref = rest[n_cast:n_cast + 3]
    _side_cast(rest[:n_cast], rest[n_cast + 3:])
    x = x_ref[...]
    inv = lax.rsqrt(jnp.mean(x * x, axis=-1, keepdims=True) + EPS)
    h = (x * g_ref[...]).astype(BF16)

    def block(c):
        return jnp.dot(h, w_ref[:, c * IN_COL_TILE:(c + 1) * IN_COL_TILE],
                       preferred_element_type=F32)

    for c in range(2):
        acc = block(c)
        cos = cos_ref[c] * inv
        sin = sin_ref[c] * inv
        for h_i in range(RET_HEADS):
            lo = c * RET_WIDTH + h_i * RET_HEAD_DIM
            a = acc[:, h_i * RET_HEAD_DIM:(h_i + 1) * RET_HEAD_DIM]
            qk_ref[:, lo:lo + RET_HEAD_DIM] = (
                a * cos + pltpu.roll(a, RET_HEAD_DIM // 2, axis=1) * sin).astype(BF16)
    for c in range(2):
        vg_ref[:, c * RET_WIDTH:(c + 1) * RET_WIDTH] = (block(2 + c) * inv).astype(BF16)
    acc = block(4) * inv
    for k in range(SSM_SLABS):
        u_ref[k] = acc[:, k * LANES:(k + 1) * LANES]


def _inproj(x2d, gain, w_bf16, rope, seq, cast=()):
    m = x2d.shape[0]
    assert IN_COL_TILE == RET_WIDTH == SSM_WIDTH
    tiles_per_seq = seq // ROW_TILE
    tab = pl.BlockSpec((2, ROW_TILE, RET_HEAD_DIM), lambda i: (0, i % tiles_per_seq, 0))
    row2 = pl.BlockSpec((ROW_TILE, 2 * RET_WIDTH), lambda i: (i, 0))
    cast_specs = _side_cast_specs(cast, m // ROW_TILE, lambda i: i)
    outs = pl.pallas_call(
        functools.partial(_inproj_kernel, n_cast=len(cast)),
        grid=(m // ROW_TILE,),
        in_specs=[
            pl.BlockSpec((ROW_TILE, D_MODEL), lambda i: (i, 0)),
            pl.BlockSpec((1, D_MODEL), lambda i: (0, 0)),
            pl.BlockSpec((D_MODEL, IN_WIDTH), lambda i: (0, 0), pipeline_mode=pl.Buffered(1)),
            tab, tab,
        ] + cast_specs,
        out_specs=[row2, row2,
                   pl.BlockSpec((SSM_SLABS, ROW_TILE, LANES), lambda i: (0, i, 0))] + cast_specs,
        out_shape=[jax.ShapeDtypeStruct((m, 2 * RET_WIDTH), BF16),
                   jax.ShapeDtypeStruct((m, 2 * RET_WIDTH), BF16),
                   jax.ShapeDtypeStruct((SSM_SLABS, m, LANES), F32)]
        + [jax.ShapeDtypeStruct(w.shape, BF16) for w in cast],
        compiler_params=_params(("parallel",)),
        name="inproj",
    )(x2d, gain, w_bf16, *rope, *cast)
    return outs[:3], outs[3:]


def _retention_kernel(q_ref, k_ref, v_ref, g_ref, dmat_ref, qdec_ref, kdec_ref,
                      gn_ref, *rest, block_decay, n_cast):
    o_ref, st_ref = rest[n_cast], rest[-1]
    _side_cast(rest[:n_cast], rest[n_cast + 1:-1])

    @pl.when(pl.program_id(1) == 0)
    def _():
        st_ref[...] = jnp.zeros_like(st_ref)

    for h in range(RET_HEADS):
        hs = slice(h * RET_HEAD_DIM, (h + 1) * RET_HEAD_DIM)
        q = q_ref[:, hs]
        k = k_ref[:, hs]
        v = v_ref[:, hs]
        s = lax.dot_general(q, k, (((1,), (1,)), ((), ())),
                            preferred_element_type=F32) * dmat_ref[h]
        out = jnp.dot(s.astype(BF16), v, preferred_element_type=F32)
        state = st_ref[h]
        out = out + jnp.dot(q, state.astype(BF16), preferred_element_type=F32) * qdec_ref[h]
        kv = lax.dot_general(k * kdec_ref[h].astype(BF16), v, (((0,), (0,)), ((), ())),
                             preferred_element_type=F32)
        st_ref[h] = state * block_decay[h] + kv
        mu = jnp.mean(out, axis=-1, keepdims=True)
        cen = out - mu
        var = jnp.mean(cen * cen, axis=-1, keepdims=True)
        normed = cen * lax.rsqrt(var + EPS) * gn_ref[:, hs]
        gate = g_ref[:, hs].astype(F32)
        o_ref[:, hs] = (jax.nn.silu(gate) * normed).astype(BF16)


def _retention_log_decay():
    return np.log1p(-(2.0 ** (-5.0 - np.arange(RET_HEADS, dtype=np.float64))))


def _retention_tables():
    log_g = _retention_log_decay()
    idx = np.arange(RET_BLOCK, dtype=np.float64)
    chunk = np.arange(RET_BLOCK) // CHUNK
    diff = idx[:, None] - idx[None, :]
    same = chunk[:, None] == chunk[None, :]
    earlier = chunk[None, :] < chunk[:, None]
    dist = np.where(same, np.abs(diff), diff)
    dmat = np.where((same | earlier)[None], np.exp(log_g[:, None, None] * dist[None]), 0.0)
    qdec = np.exp(log_g[:, None] * (idx + 1.0)[None, :])
    kdec = np.exp(log_g[:, None] * (RET_BLOCK - 1.0 - idx)[None, :])
    qdec = np.broadcast_to(qdec[:, :, None], (RET_HEADS, RET_BLOCK, RET_HEAD_DIM))
    kdec = np.broadcast_to(kdec[:, :, None], (RET_HEADS, RET_BLOCK, RET_HEAD_DIM))
    return tuple(jnp.asarray(a, dtype=F32) for a in (dmat, qdec, kdec))


def _retention(qk, vg, gn_gain, batch, seq, cast=()):
    m = qk.shape[0]
    nblk = seq // RET_BLOCK
    cast_specs = _side_cast_specs(cast, batch * nblk, lambda b, t: b * nblk + t)
    dmat, qdec, kdec = _retention_tables()
    block_decay = tuple(float(v) for v in np.exp(_retention_log_decay() * RET_BLOCK))

    def col(c):
        return pl.BlockSpec((RET_BLOCK, RET_WIDTH), lambda b, t, c=c: (b * nblk + t, c))

    def whole(shape):
        return pl.BlockSpec(shape, lambda b, t: (0,) * len(shape))

    outs = pl.pallas_call(
        functools.partial(_retention_kernel, block_decay=block_decay, n_cast=len(cast)),
        grid=(batch, nblk),
        in_specs=[
            col(0), col(1), col(0), col(1),
            whole((RET_HEADS, RET_BLOCK, RET_BLOCK)),
            whole((RET_HEADS, RET_BLOCK, RET_HEAD_DIM)),
            whole((RET_HEADS, RET_BLOCK, RET_HEAD_DIM)),
            whole((1, RET_WIDTH)),
        ] + cast_specs,
        out_specs=[pl.BlockSpec((RET_BLOCK, RET_WIDTH), lambda b, t: (b * nblk + t, 0))]
        + cast_specs,
        out_shape=[jax.ShapeDtypeStruct((m, RET_WIDTH), BF16)]
        + [jax.ShapeDtypeStruct(w.shape, BF16) for w in cast],
        scratch_shapes=[pltpu.VMEM((RET_HEADS, RET_HEAD_DIM, RET_HEAD_DIM), F32)],
        compiler_params=_params(("parallel", "arbitrary")),
        name="retention",
    )(qk, qk, vg, vg, dmat, qdec, kdec, gn_gain, *cast)
    return outs[0], outs[1:]


def _cmul(ar, ai, br, bi):
    return ar * br - ai * bi, ar * bi + ai * br


def _zoh(a_re, a_im, dt):
    e = jnp.exp(a_re * dt)
    lr = e * jnp.cos(a_im * dt)
    li = e * jnp.sin(a_im * dt)
    inv = 1.0 / (a_re * a_re + a_im * a_im)
    xr = lr - 1.0
    return lr, li, (xr * a_re + li * a_im) * inv, (li * a_re - xr * a_im) * inv


def _powers_by_bits(expo, lr, li, n_bits):
    pr = jnp.ones(expo.shape, F32)
    pi = jnp.zeros(expo.shape, F32)
    qr, qi = lr, li
    for bit in range(n_bits):
        sel = (expo & (1 << bit)) != 0
        mr, mi = _cmul(pr, pi, qr, qi)
        pr = jnp.where(sel, mr, pr)
        pi = jnp.where(sel, mi, pi)
        qr, qi = _cmul(qr, qi, qr, qi)
    return pr, pi, qr, qi


S5_PREP_INPUTS = 10
S5_PREP_OUTPUTS = 5


def _s5_prep_kernel(*refs, n_cast):
    ins = refs[:S5_PREP_INPUTS]
    outs = refs[S5_PREP_INPUTS + n_cast:S5_PREP_INPUTS + n_cast + S5_PREP_OUTPUTS]
    _side_cast(refs[S5_PREP_INPUTS:S5_PREP_INPUTS + n_cast],
               refs[S5_PREP_INPUTS + n_cast + S5_PREP_OUTPUTS:])
    for g in range(SSM_GB):
        _s5_prep_group(*[r.at[g] for r in ins + outs])


def _s5_prep_group(logdt_ref, are_l_ref, aim_l_ref, are_s_ref, aim_s_ref, bre_ref, bim_ref,
                   cre_ref, cim_ref, d_ref, w_ref, m_ref, v_ref, da_ref, db_ref):
    n_bits = SSM_T.bit_length() - 1
    dt = jnp.exp(logdt_ref[...])

    lr, li, br, bi = _zoh(are_l_ref[...], aim_l_ref[...], dt)
    bbr, bbi = _cmul(br, bi, bre_ref[...], bim_ref[...])
    low = lax.broadcasted_iota(jnp.int32, (SSM_GROUP, LANES), 1) < SSM_STATE
    x1 = jnp.where(low, bbr, bbi)
    x2 = jnp.where(low, -bbi, bbr)
    row = lax.broadcasted_iota(jnp.int32, (SSM_CW, LANES), 0)
    pr, pi, lr_t, li_t = _powers_by_bits(SSM_T - 1 - row // SSM_GROUP, lr, li, n_bits)
    w = pr * jnp.concatenate([x1] * SSM_T, axis=0) + pi * jnp.concatenate([x2] * SSM_T, axis=0)
    w_ref[:, :SSM_SW] = w.astype(BF16)
    w_ref[:, SSM_SW:] = pltpu.roll(w, SSM_STATE, axis=1).astype(BF16)
    da_ref[...] = lr_t
    db_ref[...] = jnp.where(low[:1], -li_t, li_t)

    lrs, lis, _, _ = _zoh(are_s_ref[...], aim_s_ref[...], dt)
    tau = lax.broadcasted_iota(jnp.int32, (SSM_STATE, SSM_CW), 1) // SSM_GROUP
    p0r, p0i, _, _ = _powers_by_bits(tau, lrs, lis, n_bits)
    cr, ci = cre_ref[...], cim_ref[...]
    gr, gi = _cmul(cr, ci, p0r, p0i)
    lhs = jnp.where(low, bbr, -bbi)
    k_all = jnp.dot(lhs, jnp.concatenate([gr, gi], axis=0),
                    preferred_element_type=F32, precision=lax.Precision.HIGHEST)
    crow = lax.broadcasted_iota(jnp.int32, (SSM_GROUP, SSM_CW), 0)
    clane = lax.broadcasted_iota(jnp.int32, (SSM_GROUP, SSM_CW), 1)
    k_all = k_all + jnp.where(crow == clane, d_ref[...], 0.0)
    for s in range(SSM_T):
        shifted = k_all if s == 0 else pltpu.roll(k_all, s * SSM_GROUP, axis=1)
        m_ref[s * SSM_GROUP:(s + 1) * SSM_GROUP, :] = jnp.where(
            clane >= s * SSM_GROUP, shifted, 0.0).astype(BF16)
    p1r, p1i = _cmul(p0r, p0i, lrs, lis)
    vr, vi = _cmul(cr, ci, p1r, p1i)
    v_ref[:SSM_STATE, :] = vr.astype(BF16)
    v_ref[SSM_STATE:, :] = (-vi).astype(BF16)


def _s5_prep(a_re, a_im, log_dt, b_re, b_im, c_re, c_im, d_skip, cast=()):
    g = SSM_GROUPS
    cast_specs = _side_cast_specs(cast, g // SSM_GB, lambda i: i)
    dup = lambda a: jnp.concatenate([a, a], axis=-1)
    args = (
        log_dt.reshape(g, 1, 1),
        dup(a_re)[:, None, :], dup(a_im)[:, None, :],
        a_re[:, :, None], a_im[:, :, None],
        dup(b_re.transpose(0, 2, 1)), dup(b_im.transpose(0, 2, 1)),
        jnp.tile(c_re.transpose(0, 2, 1), (1, 1, SSM_T)),
        jnp.tile(c_im.transpose(0, 2, 1), (1, 1, SSM_T)),
        d_skip.reshape(g, SSM_GROUP, 1),
    )

    def spec(a):
        return pl.BlockSpec((SSM_GB,) + a.shape[1:], lambda i: (i, 0, 0))

    out_shapes = [
        jax.ShapeDtypeStruct((g, SSM_CW, 2 * SSM_SW), BF16),
        jax.ShapeDtypeStruct((g, SSM_CW, SSM_CW), BF16),
        jax.ShapeDtypeStruct((g, SSM_SW, SSM_CW), BF16),
        jax.ShapeDtypeStruct((g, 1, SSM_SW), F32),
        jax.ShapeDtypeStruct((g, 1, SSM_SW), F32),
    ]
    assert len(args) == S5_PREP_INPUTS and len(out_shapes) == S5_PREP_OUTPUTS
    outs = pl.pallas_call(
        functools.partial(_s5_prep_kernel, n_cast=len(cast)),
        grid=(g // SSM_GB,),
        in_specs=[spec(a) for a in args] + cast_specs,
        out_specs=[spec(s) for s in out_shapes] + cast_specs,
        out_shape=out_shapes + [jax.ShapeDtypeStruct(w.shape, BF16) for w in cast],
        compiler_params=_params(("parallel",)),
        name="s5_prep",
    )(*args, *cast)
    wmat, mmat, vmat, dec_a, dec_b = outs[:S5_PREP_OUTPUTS]
    mats = (wmat, mmat, vmat, dec_a.reshape(g, SSM_SW), dec_b.reshape(g, SSM_SW))
    return mats, outs[S5_PREP_OUTPUTS:]


def _chunk_permutation():
    n = SSM_GB * LANES
    t, g, c = np.meshgrid(np.arange(SSM_GB), np.arange(SSM_GB), np.arange(SSM_GROUP),
                          indexing="ij")
    p = np.zeros((n, n), np.float32)
    p[(t * LANES + g * SSM_GROUP + c).ravel(), (g * LANES + t * SSM_GROUP + c).ravel()] = 1.0
    return jnp.asarray(p, dtype=BF16)


def _s5_kernel(u_ref, p_ref, w_ref, m_ref, v_ref, da_ref, db_ref, *rest, n_chunks, n_cast):
    y_ref = rest[n_cast]
    cat_ref, uc_ref, wx_ref, ws_ref, xp_ref = rest[2 * n_cast + 1:]
    _side_cast(rest[:n_cast], rest[n_cast + 1:2 * n_cast + 1])
    halves = SSM_T // SSM_GB
    half_w = SSM_GB * LANES

    for t in range(SSM_T):
        cat_ref[:, t * LANES:(t + 1) * LANES] = (
            u_ref[pl.ds(t, n_chunks, stride=SSM_T), :].astype(BF16))
    for half in range(halves):
        z = jnp.dot(cat_ref[:, half * half_w:(half + 1) * half_w], p_ref[...],
                    preferred_element_type=F32).astype(BF16)
        for g in range(SSM_GB):
            uc_ref[g, :, half * LANES:(half + 1) * LANES] = z[:, g * LANES:(g + 1) * LANES]

    for g in range(SSM_GB):
        w = jnp.dot(uc_ref[g], w_ref[g], preferred_element_type=F32)
        wx_ref[pl.ds(g, n_chunks, stride=SSM_GB), :] = w[:, :SSM_SW]
        ws_ref[pl.ds(g, n_chunks, stride=SSM_GB), :] = w[:, SSM_SW:]

    dec_a = da_ref[...]
    dec_b = db_ref[...]

    def step(n, carry):
        x, xs = carry
        rows = pl.ds(pl.multiple_of(n * SSM_GB, SSM_GB), SSM_GB)
        xp_ref[rows, :] = x
        return (dec_a * x + dec_b * xs + wx_ref[rows, :],
                dec_a * xs - dec_b * x + ws_ref[rows, :])

    zero = jnp.zeros((SSM_GB, SSM_SW), F32)
    lax.fori_loop(0, n_chunks, step, (zero, zero), unroll=8)

    for g in range(SSM_GB):
        y = jnp.dot(uc_ref[g], m_ref[g], preferred_element_type=F32)
        xp = xp_ref[pl.ds(g, n_chunks, stride=SSM_GB), :].astype(BF16)
        y = (y + jnp.dot(xp, v_ref[g], preferred_element_type=F32)).astype(BF16)
        for half in range(halves):
            cat_ref[:, half * half_w + g * LANES:half * half_w + (g + 1) * LANES] = (
                y[:, half * LANES:(half + 1) * LANES])

    for half in range(halves):
        z = jnp.dot(cat_ref[:, half * half_w:(half + 1) * half_w], p_ref[...],
                    preferred_element_type=F32)
        for t in range(SSM_GB):
            y_ref[pl.ds(half * SSM_GB + t, n_chunks, stride=SSM_T), :] = (
                z[:, t * LANES:(t + 1) * LANES])


def _s5(u_slabs, mats, batch, seq, cast=()):
    wmat, mmat, vmat, dec_a, dec_b = mats
    m = u_slabs.shape[1]
    n_chunks = seq // SSM_T
    perm = _chunk_permutation()
    cast_specs = _side_cast_specs(cast, SSM_SLABS * batch, lambda k, b: k * batch + b)

    def grp(shape):
        return pl.BlockSpec((SSM_GB,) + shape, lambda k, b: (k,) + (0,) * len(shape))

    seq_spec = pl.BlockSpec((None, seq, LANES), lambda k, b: (k, b, 0))
    outs = pl.pallas_call(
        functools.partial(_s5_kernel, n_chunks=n_chunks, n_cast=len(cast)),
        grid=(SSM_SLABS, batch),
        in_specs=[
            seq_spec,
            pl.BlockSpec(perm.shape, lambda k, b: (0, 0)),
            grp((SSM_CW, 2 * SSM_SW)),
            grp((SSM_CW, SSM_CW)),
            grp((SSM_SW, SSM_CW)),
            grp((SSM_SW,)),
            grp((SSM_SW,)),
        ] + cast_specs,
        out_specs=[seq_spec] + cast_specs,
        out_shape=[jax.ShapeDtypeStruct((SSM_SLABS, m, LANES), F32)]
        + [jax.ShapeDtypeStruct(w.shape, BF16) for w in cast],
        scratch_shapes=[
            pltpu.VMEM((n_chunks, SSM_T * LANES), BF16),
            pltpu.VMEM((SSM_GB, n_chunks, SSM_CW), BF16),
            pltpu.VMEM((SSM_GB * n_chunks, SSM_SW), F32),
            pltpu.VMEM((SSM_GB * n_chunks, SSM_SW), F32),
            pltpu.VMEM((SSM_GB * n_chunks, SSM_SW), F32),
        ],
        compiler_params=_params(("parallel", "parallel")),
        name="s5_chunks",
    )(u_slabs, perm, wmat, mmat, vmat, dec_a, dec_b, *cast)
    return outs[0], outs[1:]


def _outproj_kernel(x_ref, yr_ref, ys_ref, wglu_ref, bglu_ref, og_ref, wout_ref, *rest, n_cast):
    o_ref = rest[n_cast]
    _side_cast(rest[:n_cast], rest[n_cast + 1:])
    y1 =jax.nn.gelu(jnp.concatenate([ys_ref[k] for k in range(SSM_SLABS)], axis=-1))
    z = jnp.dot(y1.astype(BF16), wglu_ref[...], preferred_element_type=F32) + bglu_ref[...]
    y2 = y1 * jax.nn.sigmoid(z)
    y_ssm = _rms(y2, og_ref[...]).astype(BF16)
    acc = jnp.dot(yr_ref[...], wout_ref[:RET_WIDTH, :], preferred_element_type=F32)
    acc = acc + jnp.dot(y_ssm, wout_ref[RET_WIDTH:, :], preferred_element_type=F32)
    o_ref[...] = x_ref[...] + acc


def _outproj(x2d, y_ret, y_s5, w_glu, b_glu, out_g, w_out, cast=()):
    m = x2d.shape[0]
    cast_specs = _side_cast_specs(cast, m // ROW_TILE, lambda i: i)

    def rows(width):
        return pl.BlockSpec((ROW_TILE, width), lambda i: (i, 0))

    def whole(shape):
        return pl.BlockSpec(shape, lambda i: (0, 0))

    outs = pl.pallas_call(
        functools.partial(_outproj_kernel, n_cast=len(cast)),
        grid=(m // ROW_TILE,),
        in_specs=[
            rows(D_MODEL), rows(RET_WIDTH),
            pl.BlockSpec((SSM_SLABS, ROW_TILE, LANES), lambda i: (0, i, 0)),
            whole((SSM_WIDTH, SSM_WIDTH)), whole((1, SSM_WIDTH)), whole((1, SSM_WIDTH)),
            whole((D_MODEL, D_MODEL)),
        ] + cast_specs,
        out_specs=[rows(D_MODEL)] + cast_specs,
        out_shape=[jax.ShapeDtypeStruct((m, D_MODEL), F32)]
        + [jax.ShapeDtypeStruct(w.shape, BF16) for w in cast],
        compiler_params=_params(("parallel",)),
        name="outproj",
    )(x2d, y_ret, y_s5, w_glu, b_glu, out_g, w_out, *cast)
    return outs[0], outs[1:]


def _ffn_kernel(x_ref, g_ref, wg_ref, wu_ref, wd_ref, gf_ref, o_ref, h_ref):
    j = pl.program_id(1)
    last = pl.num_programs(1) - 1

    def step(first, final):
        if first:
            x = x_ref[...]
            h = _rms(x, g_ref[...]).astype(BF16)
            h_ref[...] = h
            base = x
        else:
            h = h_ref[...]
            base = o_ref[...]
        gate = jnp.dot(h, wg_ref[...], preferred_element_type=F32)
        up = jnp.dot(h, wu_ref[...], preferred_element_type=F32)
        act = (jax.nn.silu(gate) * up).astype(BF16)
        o = base + jnp.dot(act, wd_ref[...], preferred_element_type=F32)
        o_ref[...] = _rms(o, gf_ref[...]) if final else o

    pl.when(j == 0)(lambda: step(True, False))
    pl.when((j > 0) & (j < last))(lambda: step(False, False))
    pl.when(j == last)(lambda: step(False, True))


def _ffn(x2d, gain, w_gate, w_up, w_down, gain_final):
    m = x2d.shape[0]
    return pl.pallas_call(
        _ffn_kernel,
        grid=(m // FFN_ROW_TILE, D_FF // FF_TILE),
        in_specs=[
            pl.BlockSpec((FFN_ROW_TILE, D_MODEL), lambda i, j: (i, 0)),
            pl.BlockSpec((1, D_MODEL), lambda i, j: (0, 0)),
            pl.BlockSpec((D_MODEL, FF_TILE), lambda i, j: (0, j)),
            pl.BlockSpec((D_MODEL, FF_TILE), lambda i, j: (0, j)),
            pl.BlockSpec((FF_TILE, D_MODEL), lambda i, j: (j, 0)),
            pl.BlockSpec((1, D_MODEL), lambda i, j: (0, 0)),
        ],
        out_specs=pl.BlockSpec((FFN_ROW_TILE, D_MODEL), lambda i, j: (i, 0)),
        out_shape=jax.ShapeDtypeStruct((m, D_MODEL), F32),
        scratch_shapes=[pltpu.VMEM((FFN_ROW_TILE, D_MODEL), BF16)],
        compiler_params=_params(("parallel", "arbitrary")),
        name="ffn",
    )(x2d, gain, w_gate, w_up, w_down, gain_final)


def _rope_tables(seq):
    half = RET_HEAD_DIM // 2
    pos = np.arange(seq, dtype=np.float64)
    freqs = ROPE_BASE ** (-np.arange(half, dtype=np.float64) / half)
    ang = pos[:, None] * freqs[None, :]
    cos = np.concatenate([np.cos(ang), np.cos(ang)], axis=-1)
    sin = np.concatenate([-np.sin(ang), np.sin(ang)], axis=-1)
    scale = np.array([1.0, RET_HEAD_DIM ** -0.5])[:, None, None]
    return (jnp.asarray(cos[None] * scale, dtype=F32), jnp.asarray(sin[None] * scale, dtype=F32))


def kernel(x, norm_mix_g, w_in, ret_gn_g, ssm_a_re, ssm_a_im, ssm_log_dt, ssm_b_re, ssm_b_im,
           ssm_c_re, ssm_c_im, ssm_d, ssm_w_glu, ssm_b_glu, ssm_out_g, w_out, norm_ffn_g,
           w_gate, w_up, w_down, norm_final_g):
    batch, seq, d = x.shape
    depth = w_in.shape[0]
    assert d == D_MODEL and seq % ROW_TILE == 0 and seq % RET_BLOCK == 0 and seq % SSM_T == 0
    m = batch * seq
    rope = _rope_tables(seq)
    x2d = x.reshape(m, d)

    for l in range(depth):
        mats, (w_in_b,) = _s5_prep(ssm_a_re[l], ssm_a_im[l], ssm_log_dt[l], ssm_b_re[l],
                                   ssm_b_im[l], ssm_c_re[l], ssm_c_im[l], ssm_d[l],
                                   cast=(w_in[l],))
        (qk, vg, u_slabs), (w_gate_b,) = _inproj(x2d, norm_mix_g[l][None], w_in_b, rope, seq,
                                                 cast=(w_gate[l],))
        y_ret, (w_out_b, w_glu_b) = _retention(qk, vg, ret_gn_g[l][None], batch, seq,
                                               cast=(w_out[l], ssm_w_glu[l]))
        y_s5, (w_down_b,) = _s5(u_slabs, mats, batch, seq, cast=(w_down[l],))

        x2d, (w_up_b,) = _outproj(x2d, y_ret, y_s5, w_glu_b, ssm_b_glu[l][None],
                                  ssm_out_g[l][None], w_out_b, cast=(w_up[l],))
        last = l == depth - 1
        assert last, "fused final norm assumes a single layer"
        x2d = _ffn(x2d, norm_ffn_g[l][None], w_gate_b, w_up_b, w_down_b, norm_final_g[None])
    return x2d.reshape(batch, seq, d)
```

```python
import functools
import math

import jax
import jax.numpy as jnp
import numpy as np
from jax import lax
from jax.experimental import pallas as pl
from jax.experimental.pallas import tpu as pltpu

D_MODEL = 2048
CHUNK = 64
RET_WIDTH = D_MODEL // 2
RET_HEADS = 8
RET_HEAD_DIM = RET_WIDTH // RET_HEADS
SSM_WIDTH = D_MODEL - RET_WIDTH
SSM_GROUP = 16
SSM_GROUPS = SSM_WIDTH // SSM_GROUP
SSM_STATE = 64
D_FF = -(-8 * D_MODEL // (3 * 256)) * 256
IN_WIDTH = 4 * RET_WIDTH + SSM_WIDTH
ROPE_BASE = 10000.0
EPS = 1e-6

F32 = jnp.float32
BF16 = jnp.bfloat16

V7X_VMEM_BYTES = 64 * 1024 * 1024
VMEM_LIMIT = V7X_VMEM_BYTES - 4 * 1024 * 1024

ROW_TILE = 512
IN_COL_TILE = RET_WIDTH
FFN_ROW_TILE = 1024
FF_TILE = 512
FF_SPLIT = 2
RET_BLOCK = 256
SSM_T = 16
SSM_CW = SSM_T * SSM_GROUP
LANES = 128
SUBLANES = 8
BF16_ROWS = 16
SSM_GB = LANES // SSM_GROUP
SSM_SLABS = SSM_GROUPS // SSM_GB
SSM_SW = 2 * SSM_STATE


def _params(sem):
    return pltpu.CompilerParams(dimension_semantics=sem, vmem_limit_bytes=VMEM_LIMIT)


def _rms(x, g):
    return x * lax.rsqrt(jnp.mean(x * x, axis=-1, keepdims=True) + EPS) * g


def _side_cast_specs(weights, n_steps, step_of):
    specs = []
    for w in weights:
        rows, cols = w.shape
        assert rows % (n_steps * BF16_ROWS) == 0, (w.shape, n_steps)
        specs.append(pl.BlockSpec((rows // n_steps, cols),
                                  lambda *idx, step_of=step_of: (step_of(*idx), 0)))
    return specs


def _side_cast(srcs, dsts):
    for src, dst in zip(srcs, dsts):
        dst[...] = src[...].astype(BF16)


def _inproj_kernel(x_ref, g_ref, w_ref, cos_ref, sin_ref, *rest, n_cast):
    qk_ref, vg_ref, u_ref = rest[n_cast:n_cast + 3]
    _side_cast(rest[:n_cast], rest[n_cast + 3:])
    x = x_ref[...]
    inv = lax.rsqrt(jnp.mean(x * x, axis=-1, keepdims=True) + EPS)
    h = (x * g_ref[...]).astype(BF16)

    def block(c):
        return jnp.dot(h, w_ref[:, c * IN_COL_TILE:(c + 1) * IN_COL_TILE],
                       preferred_element_type=F32)

    for c in range(2):
        acc = block(c)
        cos = cos_ref[c] * inv
        sin = sin_ref[c] * inv
        for h_i in range(RET_HEADS):
            lo = c * RET_WIDTH + h_i * RET_HEAD_DIM
            a = acc[:, h_i * RET_HEAD_DIM:(h_i + 1) * RET_HEAD_DIM]
            qk_ref[:, lo:lo + RET_HEAD_DIM] = (
                a * cos + pltpu.roll(a, RET_HEAD_DIM // 2, axis=1) * sin).astype(BF16)
    for c in range(2):
        vg_ref[:, c * RET_WIDTH:(c + 1) * RET_WIDTH] = (block(2 + c) * inv).astype(BF16)
    acc = block(4) * inv
    for k in range(SSM_SLABS):
        u_ref[k] = acc[:, k * LANES:(k + 1) * LANES]


def _inproj(x2d, gain, w_bf16, rope, seq, cast=()):
    m = x2d.shape[0]
    assert IN_COL_TILE == RET_WIDTH == SSM_WIDTH
    tiles_per_seq = seq // ROW_TILE
    tab = pl.BlockSpec((2, ROW_TILE, RET_HEAD_DIM), lambda i: (0, i % tiles_per_seq, 0))
    row2 = pl.BlockSpec((ROW_TILE, 2 * RET_WIDTH), lambda i: (i, 0))
    cast_specs = _side_cast_specs(cast, m // ROW_TILE, lambda i: i)
    outs = pl.pallas_call(
        functools.partial(_inproj_kernel, n_cast=len(cast)),
        grid=(m // ROW_TILE,),
        in_specs=[
            pl.BlockSpec((ROW_TILE, D_MODEL), lambda i: (i, 0)),
            pl.BlockSpec((1, D_MODEL), lambda i: (0, 0)),
            pl.BlockSpec((D_MODEL, IN_WIDTH), lambda i: (0, 0), pipeline_mode=pl.Buffered(1)),
            tab, tab,
        ] + cast_specs,
        out_specs=[row2, row2,
                   pl.BlockSpec((SSM_SLABS, ROW_TILE, LANES), lambda i: (0, i, 0))] + cast_specs,
        out_shape=[jax.ShapeDtypeStruct((m, 2 * RET_WIDTH), BF16),
                   jax.ShapeDtypeStruct((m, 2 * RET_WIDTH), BF16),
                   jax.ShapeDtypeStruct((SSM_SLABS, m, LANES), F32)]
        + [jax.ShapeDtypeStruct(w.shape, BF16) for w in cast],
        compiler_params=_params(("parallel",)),
        name="inproj",
    )(x2d, gain, w_bf16, *rope, *cast)
    return outs[:3], outs[3:]


def _retention_kernel(q_ref, k_ref, v_ref, g_ref, dmat_ref, qdec_ref, kdec_ref,
                      gn_ref, *rest, block_decay, n_cast):
    o_ref, st_ref = rest[n_cast], rest[-1]
    _side_cast(rest[:n_cast], rest[n_cast + 1:-1])

    @pl.when(pl.program_id(1) == 0)
    def _():
        st_ref[...] = jnp.zeros_like(st_ref)

    for h in range(RET_HEADS):
        hs = slice(h * RET_HEAD_DIM, (h + 1) * RET_HEAD_DIM)
        q = q_ref[:, hs]
        k = k_ref[:, hs]
        v = v_ref[:, hs]
        s = lax.dot_general(q, k, (((1,), (1,)), ((), ())),
                            preferred_element_type=F32) * dmat_ref[h]
        out = jnp.dot(s.astype(BF16), v, preferred_element_type=F32)
        state = st_ref[h]
        out = out + jnp.dot(q, state.astype(BF16), preferred_element_type=F32) * qdec_ref[h]
        kv = lax.dot_general(k * kdec_ref[h].astype(BF16), v, (((0,), (0,)), ((), ())),
                             preferred_element_type=F32)
        st_ref[h] = state * block_decay[h] + kv
        mu = jnp.mean(out, axis=-1, keepdims=True)
        cen = out - mu
        var = jnp.mean(cen * cen, axis=-1, keepdims=True)
        normed = cen * lax.rsqrt(var + EPS) * gn_ref[:, hs]
        gate = g_ref[:, hs].astype(F32)
        o_ref[:, hs] = (jax.nn.silu(gate) * normed).astype(BF16)


def _retention_log_decay():
    return np.log1p(-(2.0 ** (-5.0 - np.arange(RET_HEADS, dtype=np.float64))))


def _retention_tables():
    log_g = _retention_log_decay()
    idx = np.arange(RET_BLOCK, dtype=np.float64)
    chunk = np.arange(RET_BLOCK) // CHUNK
    diff = idx[:, None] - idx[None, :]
    same = chunk[:, None] == chunk[None, :]
    earlier = chunk[None, :] < chunk[:, None]
    dist = np.where(same, np.abs(diff), diff)
    dmat = np.where((same | earlier)[None], np.exp(log_g[:, None, None] * dist[None]), 0.0)
    qdec = np.exp(log_g[:, None] * (idx + 1.0)[None, :])
    kdec = np.exp(log_g[:, None] * (RET_BLOCK - 1.0 - idx)[None, :])
    qdec = np.broadcast_to(qdec[:, :, None], (RET_HEADS, RET_BLOCK, RET_HEAD_DIM))
    kdec = np.broadcast_to(kdec[:, :, None], (RET_HEADS, RET_BLOCK, RET_HEAD_DIM))
    return tuple(jnp.asarray(a, dtype=F32) for a in (dmat, qdec, kdec))


def _retention(qk, vg, gn_gain, batch, seq, cast=()):
    m = qk.shape[0]
    nblk = seq // RET_BLOCK
    cast_specs = _side_cast_specs(cast, batch * nblk, lambda b, t: b * nblk + t)
    dmat, qdec, kdec = _retention_tables()
    block_decay = tuple(float(v) for v in np.exp(_retention_log_decay() * RET_BLOCK))

    def col(c):
        return pl.BlockSpec((RET_BLOCK, RET_WIDTH), lambda b, t, c=c: (b * nblk + t, c))

    def whole(shape):
        return pl.BlockSpec(shape, lambda b, t: (0,) * len(shape))

    outs = pl.pallas_call(
        functools.partial(_retention_kernel, block_decay=block_decay, n_cast=len(cast)),
        grid=(batch, nblk),
        in_specs=[
            col(0), col(1), col(0), col(1),
            whole((RET_HEADS, RET_BLOCK, RET_BLOCK)),
            whole((RET_HEADS, RET_BLOCK, RET_HEAD_DIM)),
            whole((RET_HEADS, RET_BLOCK, RET_HEAD_DIM)),
            whole((1, RET_WIDTH)),
        ] + cast_specs,
        out_specs=[pl.BlockSpec((RET_BLOCK, RET_WIDTH), lambda b, t: (b * nblk + t, 0))]
        + cast_specs,
        out_shape=[jax.ShapeDtypeStruct((m, RET_WIDTH), BF16)]
        + [jax.ShapeDtypeStruct(w.shape, BF16) for w in cast],
        scratch_shapes=[pltpu.VMEM((RET_HEADS, RET_HEAD_DIM, RET_HEAD_DIM), F32)],
        compiler_params=_params(("parallel", "arbitrary")),
        name="retention",
    )(qk, qk, vg, vg, dmat, qdec, kdec, gn_gain, *cast)
    return outs[0], outs[1:]


def _cmul(ar, ai, br, bi):
    return ar * br - ai * bi, ar * bi + ai * br


def _zoh(a_re, a_im, dt):
    e = jnp.exp(a_re * dt)
    lr = e * jnp.cos(a_im * dt)
    li = e * jnp.sin(a_im * dt)
    inv = 1.0 / (a_re * a_re + a_im * a_im)
    xr = lr - 1.0
    return lr, li, (xr * a_re + li * a_im) * inv, (li * a_re - xr * a_im) * inv


def _powers_by_bits(expo, lr, li, n_bits):
    pr = jnp.ones(expo.shape, F32)
    pi = jnp.zeros(expo.shape, F32)
    qr, qi = lr, li
    for bit in range(n_bits):
        sel = (expo & (1 << bit)) != 0
        mr, mi = _cmul(pr, pi, qr, qi)
        pr = jnp.where(sel, mr, pr)
        pi = jnp.where(sel, mi, pi)
        qr, qi = _cmul(qr, qi, qr, qi)
    return pr, pi, qr, qi


S5_PREP_INPUTS = 10
S5_PREP_OUTPUTS = 5


def _select_t(a, sel):
    return lax.dot_general(a, sel, (((0,), (0,)), ((), ())), preferred_element_type=F32,
                           precision=lax.Precision.HIGHEST)


def _s5_prep_kernel(*refs, n_cast):
    ins = refs[:S5_PREP_INPUTS]
    outs = refs[S5_PREP_INPUTS + n_cast:S5_PREP_INPUTS + n_cast + S5_PREP_OUTPUTS]
    _side_cast(refs[S5_PREP_INPUTS:S5_PREP_INPUTS + n_cast],
               refs[S5_PREP_INPUTS + n_cast + S5_PREP_OUTPUTS:])
    for g in range(SSM_GB):
        _s5_prep_group(*[r.at[g] for r in ins + outs])


def _s5_prep_group(logdt_ref, are_l_ref, aim_l_ref, are_s_ref, aim_s_ref, bre_ref, bim_ref,
                   cre_ref, cim_ref, d_ref, w_ref, m_ref, v_ref, da_ref, db_ref):
    n_bits = SSM_T.bit_length() - 1
    dt = jnp.exp(logdt_ref[...])

    lr, li, br, bi = _zoh(are_l_ref[...], aim_l_ref[...], dt)
    dup = (lax.broadcasted_iota(jnp.int32, (SSM_STATE, LANES), 1) % SSM_STATE
           == lax.broadcasted_iota(jnp.int32, (SSM_STATE, LANES), 0)).astype(F32)
    bbr, bbi = _cmul(br, bi, _select_t(bre_ref[...], dup), _select_t(bim_ref[...], dup))
    low = lax.broadcasted_iota(jnp.int32, (SSM_GROUP, LANES), 1) < SSM_STATE
    x1 = jnp.where(low, bbr, bbi)
    x2 = jnp.where(low, -bbi, bbr)
    row = lax.broadcasted_iota(jnp.int32, (SSM_CW, LANES), 0)
    pr, pi, lr_t, li_t = _powers_by_bits(SSM_T - 1 - row // SSM_GROUP, lr, li, n_bits)
    w = pr * jnp.concatenate([x1] * SSM_T, axis=0) + pi * jnp.concatenate([x2] * SSM_T, axis=0)
    w_ref[:, :SSM_SW] = w.astype(BF16)
    w_ref[:, SSM_SW:] = pltpu.roll(w, SSM_STATE, axis=1).astype(BF16)
    da_ref[...] = lr_t
    db_ref[...] = jnp.where(low[:1], -li_t, li_t)

    lrs, lis, _, _ = _zoh(are_s_ref[...], aim_s_ref[...], dt)
    tau = lax.broadcasted_iota(jnp.int32, (SSM_STATE, SSM_CW), 1) // SSM_GROUP
    p0r, p0i, _, _ = _powers_by_bits(tau, lrs, lis, n_bits)
    tile = (lax.broadcasted_iota(jnp.int32, (SSM_GROUP, SSM_CW), 1) % SSM_GROUP
            == lax.broadcasted_iota(jnp.int32, (SSM_GROUP, SSM_CW), 0)).astype(F32)
    cr, ci = _select_t(cre_ref[...], tile), _select_t(cim_ref[...], tile)
    gr, gi = _cmul(cr, ci, p0r, p0i)
    lhs = jnp.where(low, bbr, -bbi)
    k_all = jnp.dot(lhs, jnp.concatenate([gr, gi], axis=0),
                    preferred_element_type=F32, precision=lax.Precision.HIGHEST)
    crow = lax.broadcasted_iota(jnp.int32, (SSM_GROUP, SSM_CW), 0)
    clane = lax.broadcasted_iota(jnp.int32, (SSM_GROUP, SSM_CW), 1)
    k_all = k_all + jnp.where(crow == clane, d_ref[...], 0.0)
    for s in range(SSM_T):
        shifted = k_all if s == 0 else pltpu.roll(k_all, s * SSM_GROUP, axis=1)
        m_ref[s * SSM_GROUP:(s + 1) * SSM_GROUP, :] = jnp.where(
            clane >= s * SSM_GROUP, shifted, 0.0).astype(BF16)
    p1r, p1i = _cmul(p0r, p0i, lrs, lis)
    vr, vi = _cmul(cr, ci, p1r, p1i)
    v_ref[:SSM_STATE, :] = vr.astype(BF16)
    v_ref[SSM_STATE:, :] = (-vi).astype(BF16)


def _s5_prep(a_re, a_im, log_dt, b_re, b_im, c_re, c_im, d_skip, cast=()):
    g = SSM_GROUPS
    cast_specs = _side_cast_specs(cast, g // SSM_GB, lambda i: i)
    dup = lambda a: jnp.concatenate([a, a], axis=-1)
    args = (
        log_dt.reshape(g, 1, 1),
        dup(a_re)[:, None, :], dup(a_im)[:, None, :],
        a_re[:, :, None], a_im[:, :, None],
        b_re, b_im,
        c_re, c_im,
        d_skip.reshape(g, SSM_GROUP, 1),
    )

    def spec(a):
        return pl.BlockSpec((SSM_GB,) + a.shape[1:], lambda i: (i, 0, 0))

    out_shapes = [
        jax.ShapeDtypeStruct((g, SSM_CW, 2 * SSM_SW), BF16),
        jax.ShapeDtypeStruct((g, SSM_CW, SSM_CW), BF16),
        jax.ShapeDtypeStruct((g, SSM_SW, SSM_CW), BF16),
        jax.ShapeDtypeStruct((g, 1, SSM_SW), F32),
        jax.ShapeDtypeStruct((g, 1, SSM_SW), F32),
    ]
    assert len(args) == S5_PREP_INPUTS and len(out_shapes) == S5_PREP_OUTPUTS
    outs = pl.pallas_call(
        functools.partial(_s5_prep_kernel, n_cast=len(cast)),
        grid=(g // SSM_GB,),
        in_specs=[spec(a) for a in args] + cast_specs,
        out_specs=[spec(s) for s in out_shapes] + cast_specs,
        out_shape=out_shapes + [jax.ShapeDtypeStruct(w.shape, BF16) for w in cast],
        compiler_params=_params(("parallel",)),
        name="s5_prep",
    )(*args, *cast)
    wmat, mmat, vmat, dec_a, dec_b = outs[:S5_PREP_OUTPUTS]
    mats = (wmat, mmat, vmat, dec_a.reshape(g, SSM_SW), dec_b.reshape(g, SSM_SW))
    return mats, outs[S5_PREP_OUTPUTS:]


def _chunk_permutation():
    n = SSM_GB * LANES
    t, g, c = np.meshgrid(np.arange(SSM_GB), np.arange(SSM_GB), np.arange(SSM_GROUP),
                          indexing="ij")
    p = np.zeros((n, n), np.float32)
    p[(t * LANES + g * SSM_GROUP + c).ravel(), (g * LANES + t * SSM_GROUP + c).ravel()] = 1.0
    return jnp.asarray(p, dtype=BF16)


def _s5_kernel(u_ref, p_ref, w_ref, m_ref, v_ref, da_ref, db_ref, *rest, n_chunks, n_cast):
    y_ref = rest[n_cast]
    cat_ref, uc_ref, wx_ref, ws_ref, xp_ref = rest[2 * n_cast + 1:]
    _side_cast(rest[:n_cast], rest[n_cast + 1:2 * n_cast + 1])
    halves = SSM_T // SSM_GB
    half_w = SSM_GB * LANES

    for t in range(SSM_T):
        cat_ref[:, t * LANES:(t + 1) * LANES] = (
            u_ref[pl.ds(t, n_chunks, stride=SSM_T), :].astype(BF16))
    for half in range(halves):
        z = jnp.dot(cat_ref[:, half * half_w:(half + 1) * half_w], p_ref[...],
                    preferred_element_type=F32).astype(BF16)
        for g in range(SSM_GB):
            uc_ref[g, :, half * LANES:(half + 1) * LANES] = z[:, g * LANES:(g + 1) * LANES]

    for g in range(SSM_GB):
        w = jnp.dot(uc_ref[g], w_ref[g], preferred_element_type=F32)
        wx_ref[pl.ds(g, n_chunks, stride=SSM_GB), :] = w[:, :SSM_SW]
        ws_ref[pl.ds(g, n_chunks, stride=SSM_GB), :] = w[:, SSM_SW:]

    dec_a = da_ref[...]
    dec_b = db_ref[...]

    def step(n, carry):
        x, xs = carry
        rows = pl.ds(pl.multiple_of(n * SSM_GB, SSM_GB), SSM_GB)
        xp_ref[rows, :] = x
        return (dec_a * x + dec_b * xs + wx_ref[rows, :],
                dec_a * xs - dec_b * x + ws_ref[rows, :])

    zero = jnp.zeros((SSM_GB, SSM_SW), F32)
    lax.fori_loop(0, n_chunks, step, (zero, zero), unroll=8)

    for g in range(SSM_GB):
        y = jnp.dot(uc_ref[g], m_ref[g], preferred_element_type=F32)
        xp = xp_ref[pl.ds(g, n_chunks, stride=SSM_GB), :].astype(BF16)
        y = (y + jnp.dot(xp, v_ref[g], preferred_element_type=F32)).astype(BF16)
        for half in range(halves):
            cat_ref[:, half * half_w + g * LANES:half * half_w + (g + 1) * LANES] = (
                y[:, half * LANES:(half + 1) * LANES])

    for half in range(halves):
        z = jnp.dot(cat_ref[:, half * half_w:(half + 1) * half_w], p_ref[...],
                    preferred_element_type=F32)
        for t in range(SSM_GB):
            y_ref[pl.ds(half * SSM_GB + t, n_chunks, stride=SSM_T), :] = (
                z[:, t * LANES:(t + 1) * LANES])


def _s5(u_slabs, mats, batch, seq, cast=()):
    wmat, mmat, vmat, dec_a, dec_b = mats
    m = u_slabs.shape[1]
    n_chunks = seq // SSM_T
    perm = _chunk_permutation()
    cast_specs = _side_cast_specs(cast, SSM_SLABS * batch, lambda k, b: k * batch + b)

    def grp(shape):
        return pl.BlockSpec((SSM_GB,) + shape, lambda k, b: (k,) + (0,) * len(shape))

    seq_spec = pl.BlockSpec((None, seq, LANES), lambda k, b: (k, b, 0))
    outs = pl.pallas_call(
        functools.partial(_s5_kernel, n_chunks=n_chunks, n_cast=len(cast)),
        grid=(SSM_SLABS, batch),
        in_specs=[
            seq_spec,
            pl.BlockSpec(perm.shape, lambda k, b: (0, 0)),
            grp((SSM_CW, 2 * SSM_SW)),
            grp((SSM_CW, SSM_CW)),
            grp((SSM_SW, SSM_CW)),
            grp((SSM_SW,)),
            grp((SSM_SW,)),
        ] + cast_specs,
        out_specs=[seq_spec] + cast_specs,
        out_shape=[jax.ShapeDtypeStruct((SSM_SLABS, m, LANES), F32)]
        + [jax.ShapeDtypeStruct(w.shape, BF16) for w in cast],
        scratch_shapes=[
            pltpu.VMEM((n_chunks, SSM_T * LANES), BF16),
            pltpu.VMEM((SSM_GB, n_chunks, SSM_CW), BF16),
            pltpu.VMEM((SSM_GB * n_chunks, SSM_SW), F32),
            pltpu.VMEM((SSM_GB * n_chunks, SSM_SW), F32),
            pltpu.VMEM((SSM_GB * n_chunks, SSM_SW), F32),
        ],
        compiler_params=_params(("parallel", "parallel")),
        name="s5_chunks",
    )(u_slabs, perm, wmat, mmat, vmat, dec_a, dec_b, *cast)
    return outs[0], outs[1:]


def _outproj_kernel(x_ref, yr_ref, ys_ref, wglu_ref, bglu_ref, og_ref, wout_ref, *rest, n_cast):
    o_ref = rest[n_cast]
    _side_cast(rest[:n_cast], rest[n_cast + 1:])
    y1 =jax.nn.gelu(jnp.concatenate([ys_ref[k] for k in range(SSM_SLABS)], axis=-1))
    z = jnp.dot(y1.astype(BF16), wglu_ref[...], preferred_element_type=F32) + bglu_ref[...]
    y2 = y1 * jax.nn.sigmoid(z)
    y_ssm = _rms(y2, og_ref[...]).astype(BF16)
    acc = jnp.dot(yr_ref[...], wout_ref[:RET_WIDTH, :], preferred_element_type=F32)
    acc = acc + jnp.dot(y_ssm, wout_ref[RET_WIDTH:, :], preferred_element_type=F32)
    o_ref[...] = x_ref[...] + acc


def _outproj(x2d, y_ret, y_s5, w_glu, b_glu, out_g, w_out, cast=()):
    m = x2d.shape[0]
    cast_specs = _side_cast_specs(cast, m // ROW_TILE, lambda i: i)

    def rows(width):
        return pl.BlockSpec((ROW_TILE, width), lambda i: (i, 0))

    def whole(shape):
        return pl.BlockSpec(shape, lambda i: (0, 0))

    outs = pl.pallas_call(
        functools.partial(_outproj_kernel, n_cast=len(cast)),
        grid=(m // ROW_TILE,),
        in_specs=[
            rows(D_MODEL), rows(RET_WIDTH),
            pl.BlockSpec((SSM_SLABS, ROW_TILE, LANES), lambda i: (0, i, 0)),
            whole((SSM_WIDTH, SSM_WIDTH)), whole((1, SSM_WIDTH)), whole((1, SSM_WIDTH)),
            whole((D_MODEL, D_MODEL)),
        ] + cast_specs,
        out_specs=[rows(D_MODEL)] + cast_specs,
        out_shape=[jax.ShapeDtypeStruct((m, D_MODEL), F32)]
        + [jax.ShapeDtypeStruct(w.shape, BF16) for w in cast],
        compiler_params=_params(("parallel",)),
        name="outproj",
    )(x2d, y_ret, y_s5, w_glu, b_glu, out_g, w_out, *cast)
    return outs[0], outs[1:]


def _ffn_kernel(x_ref, g_ref, wg_ref, wu_ref, wd_ref, gf_ref, o_ref, h_ref):
    j = pl.program_id(1)
    last = pl.num_programs(1) - 1

    def step(first, final):
        if first:
            x = x_ref[...]
            h = _rms(x, g_ref[...]).astype(BF16)
            h_ref[...] = h
            base = x
        else:
            h = h_ref[...]
            base = o_ref[...]
        o = base
        acts = []
        for s in range(FF_SPLIT):
            cols = slice(s * (FF_TILE // FF_SPLIT), (s + 1) * (FF_TILE // FF_SPLIT))
            gate = jnp.dot(h, wg_ref[:, cols], preferred_element_type=F32)
            up = jnp.dot(h, wu_ref[:, cols], preferred_element_type=F32)
            acts.append((cols, (jax.nn.silu(gate) * up).astype(BF16)))
        for cols, act in acts:
            o = o + jnp.dot(act, wd_ref[cols, :], preferred_element_type=F32)
        o_ref[...] = _rms(o, gf_ref[...]) if final else o

    pl.when(j == 0)(lambda: step(True, False))
    pl.when((j > 0) & (j < last))(lambda: step(False, False))
    pl.when(j == last)(lambda: step(False, True))


def _ffn(x2d, gain, w_gate, w_up, w_down, gain_final):
    m = x2d.shape[0]
    return pl.pallas_call(
        _ffn_kernel,
        grid=(m // FFN_ROW_TILE, D_FF // FF_TILE),
        in_specs=[
            pl.BlockSpec((FFN_ROW_TILE, D_MODEL), lambda i, j: (i, 0)),
            pl.BlockSpec((1, D_MODEL), lambda i, j: (0, 0)),
            pl.BlockSpec((D_MODEL, FF_TILE), lambda i, j: (0, j)),
            pl.BlockSpec((D_MODEL, FF_TILE), lambda i, j: (0, j)),
            pl.BlockSpec((FF_TILE, D_MODEL), lambda i, j: (j, 0)),
            pl.BlockSpec((1, D_MODEL), lambda i, j: (0, 0)),
        ],
        out_specs=pl.BlockSpec((FFN_ROW_TILE, D_MODEL), lambda i, j: (i, 0)),
        out_shape=jax.ShapeDtypeStruct((m, D_MODEL), F32),
        scratch_shapes=[pltpu.VMEM((FFN_ROW_TILE, D_MODEL), BF16)],
        compiler_params=_params(("parallel", "arbitrary")),
        name="ffn",
    )(x2d, gain, w_gate, w_up, w_down, gain_final)


def _rope_tables(seq):
    half = RET_HEAD_DIM // 2
    pos = np.arange(seq, dtype=np.float64)
    freqs = ROPE_BASE ** (-np.arange(half, dtype=np.float64) / half)
    ang = pos[:, None] * freqs[None, :]
    cos = np.concatenate([np.cos(ang), np.cos(ang)], axis=-1)
    sin = np.concatenate([-np.sin(ang), np.sin(ang)], axis=-1)
    scale = np.array([1.0, RET_HEAD_DIM ** -0.5])[:, None, None]
    return (jnp.asarray(cos[None] * scale, dtype=F32), jnp.asarray(sin[None] * scale, dtype=F32))


def kernel(x, norm_mix_g, w_in, ret_gn_g, ssm_a_re, ssm_a_im, ssm_log_dt, ssm_b_re, ssm_b_im,
           ssm_c_re, ssm_c_im, ssm_d, ssm_w_glu, ssm_b_glu, ssm_out_g, w_out, norm_ffn_g,
           w_gate, w_up, w_down, norm_final_g):
    batch, seq, d = x.shape
    depth = w_in.shape[0]
    assert d == D_MODEL and seq % ROW_TILE == 0 and seq % RET_BLOCK == 0 and seq % SSM_T == 0
    m = batch * seq
    rope = _rope_tables(seq)
    x2d = x.reshape(m, d)

    for l in range(depth):
        mats, (w_in_b,) = _s5_prep(ssm_a_re[l], ssm_a_im[l], ssm_log_dt[l], ssm_b_re[l],
                                   ssm_b_im[l], ssm_c_re[l], ssm_c_im[l], ssm_d[l],
                                   cast=(w_in[l],))
        (qk, vg, u_slabs), (w_gate_b,) = _inproj(x2d, norm_mix_g[l][None], w_in_b, rope, seq,
                                                 cast=(w_gate[l],))
        y_ret, (w_out_b, w_glu_b) = _retention(qk, vg, ret_gn_g[l][None], batch, seq,
                                               cast=(w_out[l], ssm_w_glu[l]))
        y_s5, (w_down_b,) = _s5(u_slabs, mats, batch, seq, cast=(w_down[l],))

        x2d, (w_up_b,) = _outproj(x2d, y_ret, y_s5, w_glu_b, ssm_b_glu[l][None],
                                  ssm_out_g[l][None], w_out_b, cast=(w_up[l],))
        last = l == depth - 1
        assert last, "fused final norm assumes a single layer"
        x2d = _ffn(x2d, norm_ffn_g[l][None], w_gate_b, w_up_b, w_down_b, norm_final_g[None])
    return x2d.reshape(batch, seq, d)
```

```python
import functools
import math

import jax
import jax.numpy as jnp
import numpy as np
from jax import lax
from jax.experimental import pallas as pl
from jax.experimental.pallas import tpu as pltpu

D_MODEL = 2048
CHUNK = 64
RET_WIDTH = D_MODEL // 2
RET_HEADS = 8
RET_HEAD_DIM = RET_WIDTH // RET_HEADS
SSM_WIDTH = D_MODEL - RET_WIDTH
SSM_GROUP = 16
SSM_GROUPS = SSM_WIDTH // SSM_GROUP
SSM_STATE = 64
D_FF = -(-8 * D_MODEL // (3 * 256)) * 256
IN_WIDTH = 4 * RET_WIDTH + SSM_WIDTH
ROPE_BASE = 10000.0
EPS = 1e-6

F32 = jnp.float32
BF16 = jnp.bfloat16

V7X_VMEM_BYTES = 64 * 1024 * 1024
VMEM_LIMIT = V7X_VMEM_BYTES - 4 * 1024 * 1024

ROW_TILE = 512
IN_COL_TILE = RET_WIDTH
FFN_ROW_TILE = 1024
FF_TILE = 512
FF_SPLIT = 2
RET_BLOCK = 256
RET_STEP_BLOCKS = 2
SSM_T = 16
SSM_CW = SSM_T * SSM_GROUP
LANES = 128
SUBLANES = 8
BF16_ROWS = 16
SSM_GB = LANES // SSM_GROUP
SSM_SLABS = SSM_GROUPS // SSM_GB
SSM_SW = 2 * SSM_STATE


def _params(sem):
    return pltpu.CompilerParams(dimension_semantics=sem, vmem_limit_bytes=VMEM_LIMIT)


def _rms(x, g):
    return x * lax.rsqrt(jnp.mean(x * x, axis=-1, keepdims=True) + EPS) * g


def _side_cast_specs(weights, n_steps, step_of):
    specs = []
    for w in weights:
        rows, cols = w.shape
        assert rows % (n_steps * BF16_ROWS) == 0, (w.shape, n_steps)
        specs.append(pl.BlockSpec((rows // n_steps, cols),
                                  lambda *idx, step_of=step_of: (step_of(*idx), 0)))
    return specs


def _side_cast(srcs, dsts):
    for src, dst in zip(srcs, dsts):
        dst[...] = src[...].astype(BF16)


def _inproj_kernel(x_ref, g_ref, w_ref, cos_ref, sin_ref, *rest, n_cast):
    qk_ref, vg_ref, u_ref = rest[n_cast:n_cast + 3]
    _side_cast(rest[:n_cast], rest[n_cast + 3:])
    x = x_ref[...]
    inv = lax.rsqrt(jnp.mean(x * x, axis=-1, keepdims=True) + EPS)
    h = (x * g_ref[...]).astype(BF16)

    def block(c):
        return jnp.dot(h, w_ref[:, c * IN_COL_TILE:(c + 1) * IN_COL_TILE],
                       preferred_element_type=F32)

    for c in range(2):
        acc = block(c)
        cos = cos_ref[c] * inv
        sin = sin_ref[c] * inv
        for h_i in range(RET_HEADS):
            lo = c * RET_WIDTH + h_i * RET_HEAD_DIM
            a = acc[:, h_i * RET_HEAD_DIM:(h_i + 1) * RET_HEAD_DIM]
            qk_ref[:, lo:lo + RET_HEAD_DIM] = (
                a * cos + pltpu.roll(a, RET_HEAD_DIM // 2, axis=1) * sin).astype(BF16)
    for c in range(2):
        vg_ref[:, c * RET_WIDTH:(c + 1) * RET_WIDTH] = (block(2 + c) * inv).astype(BF16)
    acc = block(4) * inv
    for k in range(SSM_SLABS):
        u_ref[k] = acc[:, k * LANES:(k + 1) * LANES]


def _inproj(x2d, gain, w_bf16, rope, seq, cast=()):
    m = x2d.shape[0]
    assert IN_COL_TILE == RET_WIDTH == SSM_WIDTH
    tiles_per_seq = seq // ROW_TILE
    tab = pl.BlockSpec((2, ROW_TILE, RET_HEAD_DIM), lambda i: (0, i % tiles_per_seq, 0))
    row2 = pl.BlockSpec((ROW_TILE, 2 * RET_WIDTH), lambda i: (i, 0))
    cast_specs = _side_cast_specs(cast, m // ROW_TILE, lambda i: i)
    outs = pl.pallas_call(
        functools.partial(_inproj_kernel, n_cast=len(cast)),
        grid=(m // ROW_TILE,),
        in_specs=[
            pl.BlockSpec((ROW_TILE, D_MODEL), lambda i: (i, 0)),
            pl.BlockSpec((1, D_MODEL), lambda i: (0, 0)),
            pl.BlockSpec((D_MODEL, IN_WIDTH), lambda i: (0, 0), pipeline_mode=pl.Buffered(1)),
            tab, tab,
        ] + cast_specs,
        out_specs=[row2, row2,
                   pl.BlockSpec((SSM_SLABS, ROW_TILE, LANES), lambda i: (0, i, 0))] + cast_specs,
        out_shape=[jax.ShapeDtypeStruct((m, 2 * RET_WIDTH), BF16),
                   jax.ShapeDtypeStruct((m, 2 * RET_WIDTH), BF16),
                   jax.ShapeDtypeStruct((SSM_SLABS, m, LANES), F32)]
        + [jax.ShapeDtypeStruct(w.shape, BF16) for w in cast],
        compiler_params=_params(("parallel",)),
        name="inproj",
    )(x2d, gain, w_bf16, *rope, *cast)
    return outs[:3], outs[3:]


def _retention_kernel(q_ref, k_ref, v_ref, g_ref, dmat_ref, qdec_ref, kdec_ref,
                      gn_ref, *rest, block_decay, n_cast):
    o_ref, st_ref = rest[n_cast], rest[-1]
    _side_cast(rest[:n_cast], rest[n_cast + 1:-1])

    @pl.when(pl.program_id(1) == 0)
    def _():
        st_ref[...] = jnp.zeros_like(st_ref)

    for sub in range(RET_STEP_BLOCKS):
        rows = slice(sub * RET_BLOCK, (sub + 1) * RET_BLOCK)
        for h in range(RET_HEADS):
            hs = slice(h * RET_HEAD_DIM, (h + 1) * RET_HEAD_DIM)
            q = q_ref[rows, hs]
            k = k_ref[rows, hs]
            v = v_ref[rows, hs]
            s = lax.dot_general(q, k, (((1,), (1,)), ((), ())),
                                preferred_element_type=F32) * dmat_ref[h]
            out = jnp.dot(s.astype(BF16), v, preferred_element_type=F32)
            state = st_ref[h]
            out = out + jnp.dot(q, state.astype(BF16),
                                preferred_element_type=F32) * qdec_ref[h]
            kv = lax.dot_general(k * kdec_ref[h].astype(BF16), v, (((0,), (0,)), ((), ())),
                                 preferred_element_type=F32)
            st_ref[h] = state * block_decay[h] + kv
            mu = jnp.mean(out, axis=-1, keepdims=True)
            cen = out - mu
            var = jnp.mean(cen * cen, axis=-1, keepdims=True)
            normed = cen * lax.rsqrt(var + EPS) * gn_ref[:, hs]
            gate = g_ref[rows, hs].astype(F32)
            o_ref[rows, hs] = (jax.nn.silu(gate) * normed).astype(BF16)


def _retention_log_decay():
    return np.log1p(-(2.0 ** (-5.0 - np.arange(RET_HEADS, dtype=np.float64))))


def _retention_tables():
    log_g = _retention_log_decay()
    idx = np.arange(RET_BLOCK, dtype=np.float64)
    chunk = np.arange(RET_BLOCK) // CHUNK
    diff = idx[:, None] - idx[None, :]
    same = chunk[:, None] == chunk[None, :]
    earlier = chunk[None, :] < chunk[:, None]
    dist = np.where(same, np.abs(diff), diff)
    dmat = np.where((same | earlier)[None], np.exp(log_g[:, None, None] * dist[None]), 0.0)
    qdec = np.exp(log_g[:, None] * (idx + 1.0)[None, :])
    kdec = np.exp(log_g[:, None] * (RET_BLOCK - 1.0 - idx)[None, :])
    qdec = np.broadcast_to(qdec[:, :, None], (RET_HEADS, RET_BLOCK, RET_HEAD_DIM))
    kdec = np.broadcast_to(kdec[:, :, None], (RET_HEADS, RET_BLOCK, RET_HEAD_DIM))
    return tuple(jnp.asarray(a, dtype=F32) for a in (dmat, qdec, kdec))


def _retention(qk, vg, gn_gain, batch, seq, cast=()):
    m = qk.shape[0]
    step_rows = RET_BLOCK * RET_STEP_BLOCKS
    nblk = seq // step_rows
    cast_specs = _side_cast_specs(cast, batch * nblk, lambda b, t: b * nblk + t)
    dmat, qdec, kdec = _retention_tables()
    block_decay = tuple(float(v) for v in np.exp(_retention_log_decay() * RET_BLOCK))

    def col(c):
        return pl.BlockSpec((step_rows, RET_WIDTH), lambda b, t, c=c: (b * nblk + t, c))

    def whole(shape):
        return pl.BlockSpec(shape, lambda b, t: (0,) * len(shape))

    outs = pl.pallas_call(
        functools.partial(_retention_kernel, block_decay=block_decay, n_cast=len(cast)),
        grid=(batch, nblk),
        in_specs=[
            col(0), col(1), col(0), col(1),
            whole((RET_HEADS, RET_BLOCK, RET_BLOCK)),
            whole((RET_HEADS, RET_BLOCK, RET_HEAD_DIM)),
            whole((RET_HEADS, RET_BLOCK, RET_HEAD_DIM)),
            whole((1, RET_WIDTH)),
        ] + cast_specs,
        out_specs=[pl.BlockSpec((step_rows, RET_WIDTH), lambda b, t: (b * nblk + t, 0))]
        + cast_specs,
        out_shape=[jax.ShapeDtypeStruct((m, RET_WIDTH), BF16)]
        + [jax.ShapeDtypeStruct(w.shape, BF16) for w in cast],
        scratch_shapes=[pltpu.VMEM((RET_HEADS, RET_HEAD_DIM, RET_HEAD_DIM), F32)],
        compiler_params=_params(("parallel", "arbitrary")),
        name="retention",
    )(qk, qk, vg, vg, dmat, qdec, kdec, gn_gain, *cast)
    return outs[0], outs[1:]


def _cmul(ar, ai, br, bi):
    return ar * br - ai * bi, ar * bi + ai * br


def _zoh(a_re, a_im, dt):
    e = jnp.exp(a_re * dt)
    lr = e * jnp.cos(a_im * dt)
    li = e * jnp.sin(a_im * dt)
    inv = 1.0 / (a_re * a_re + a_im * a_im)
    xr = lr - 1.0
    return lr, li, (xr * a_re + li * a_im) * inv, (li * a_re - xr * a_im) * inv


def _powers_by_bits(expo, lr, li, n_bits):
    pr = jnp.ones(expo.shape, F32)
    pi = jnp.zeros(expo.shape, F32)
    qr, qi = lr, li
    for bit in range(n_bits):
        sel = (expo & (1 << bit)) != 0
        mr, mi = _cmul(pr, pi, qr, qi)
        pr = jnp.where(sel, mr, pr)
        pi = jnp.where(sel, mi, pi)
        qr, qi = _cmul(qr, qi, qr, qi)
    return pr, pi, qr, qi


S5_PREP_INPUTS = 10
S5_PREP_OUTPUTS = 5


def _select_t(a, sel):
    return lax.dot_general(a, sel, (((0,), (0,)), ((), ())), preferred_element_type=F32,
                           precision=lax.Precision.HIGHEST)


def _s5_prep_kernel(*refs, n_cast):
    ins = refs[:S5_PREP_INPUTS]
    outs = refs[S5_PREP_INPUTS + n_cast:S5_PREP_INPUTS + n_cast + S5_PREP_OUTPUTS]
    _side_cast(refs[S5_PREP_INPUTS:S5_PREP_INPUTS + n_cast],
               refs[S5_PREP_INPUTS + n_cast + S5_PREP_OUTPUTS:])
    for g in range(SSM_GB):
        _s5_prep_group(*[r.at[g] for r in ins + outs])


def _s5_prep_group(logdt_ref, are_l_ref, aim_l_ref, are_s_ref, aim_s_ref, bre_ref, bim_ref,
                   cre_ref, cim_ref, d_ref, w_ref, m_ref, v_ref, da_ref, db_ref):
    n_bits = SSM_T.bit_length() - 1
    dt = jnp.exp(logdt_ref[...])

    lr, li, br, bi = _zoh(are_l_ref[...], aim_l_ref[...], dt)
    dup = (lax.broadcasted_iota(jnp.int32, (SSM_STATE, LANES), 1) % SSM_STATE
           == lax.broadcasted_iota(jnp.int32, (SSM_STATE, LANES), 0)).astype(F32)
    bbr, bbi = _cmul(br, bi, _select_t(bre_ref[...], dup), _select_t(bim_ref[...], dup))
    low = lax.broadcasted_iota(jnp.int32, (SSM_GROUP, LANES), 1) < SSM_STATE
    x1 = jnp.where(low, bbr, bbi)
    x2 = jnp.where(low, -bbi, bbr)
    row = lax.broadcasted_iota(jnp.int32, (SSM_CW, LANES), 0)
    pr, pi, lr_t, li_t = _powers_by_bits(SSM_T - 1 - row // SSM_GROUP, lr, li, n_bits)
    w = pr * jnp.concatenate([x1] * SSM_T, axis=0) + pi * jnp.concatenate([x2] * SSM_T, axis=0)
    w_ref[:, :SSM_SW] = w.astype(BF16)
    w_ref[:, SSM_SW:] = pltpu.roll(w, SSM_STATE, axis=1).astype(BF16)
    da_ref[...] = lr_t
    db_ref[...] = jnp.where(low[:1], -li_t, li_t)

    lrs, lis, _, _ = _zoh(are_s_ref[...], aim_s_ref[...], dt)
    tau = lax.broadcasted_iota(jnp.int32, (SSM_STATE, SSM_CW), 1) // SSM_GROUP
    p0r, p0i, _, _ = _powers_by_bits(tau, lrs, lis, n_bits)
    tile = (lax.broadcasted_iota(jnp.int32, (SSM_GROUP, SSM_CW), 1) % SSM_GROUP
            == lax.broadcasted_iota(jnp.int32, (SSM_GROUP, SSM_CW), 0)).astype(F32)
    cr, ci = _select_t(cre_ref[...], tile), _select_t(cim_ref[...], tile)
    gr, gi = _cmul(cr, ci, p0r, p0i)
    lhs = jnp.where(low, bbr, -bbi)
    k_all = jnp.dot(lhs, jnp.concatenate([gr, gi], axis=0),
                    preferred_element_type=F32, precision=lax.Precision.HIGHEST)
    crow = lax.broadcasted_iota(jnp.int32, (SSM_GROUP, SSM_CW), 0)
    clane = lax.broadcasted_iota(jnp.int32, (SSM_GROUP, SSM_CW), 1)
    k_all = k_all + jnp.where(crow == clane, d_ref[...], 0.0)
    for s in range(SSM_T):
        shifted = k_all if s == 0 else pltpu.roll(k_all, s * SSM_GROUP, axis=1)
        m_ref[s * SSM_GROUP:(s + 1) * SSM_GROUP, :] = jnp.where(
            clane >= s * SSM_GROUP, shifted, 0.0).astype(BF16)
    p1r, p1i = _cmul(p0r, p0i, lrs, lis)
    vr, vi = _cmul(cr, ci, p1r, p1i)
    v_ref[:SSM_STATE, :] = vr.astype(BF16)
    v_ref[SSM_STATE:, :] = (-vi).astype(BF16)


def _s5_prep(a_re, a_im, log_dt, b_re, b_im, c_re, c_im, d_skip, cast=()):
    g = SSM_GROUPS
    cast_specs = _side_cast_specs(cast, g // SSM_GB, lambda i: i)
    dup = lambda a: jnp.concatenate([a, a], axis=-1)
    args = (
        log_dt.reshape(g, 1, 1),
        dup(a_re)[:, None, :], dup(a_im)[:, None, :],
        a_re[:, :, None], a_im[:, :, None],
        b_re, b_im,
        c_re, c_im,
        d_skip.reshape(g, SSM_GROUP, 1),
    )

    def spec(a):
        return pl.BlockSpec((SSM_GB,) + a.shape[1:], lambda i: (i, 0, 0))

    out_shapes = [
        jax.ShapeDtypeStruct((g, SSM_CW, 2 * SSM_SW), BF16),
        jax.ShapeDtypeStruct((g, SSM_CW, SSM_CW), BF16),
        jax.ShapeDtypeStruct((g, SSM_SW, SSM_CW), BF16),
        jax.ShapeDtypeStruct((g, 1, SSM_SW), F32),
        jax.ShapeDtypeStruct((g, 1, SSM_SW), F32),
    ]
    assert len(args) == S5_PREP_INPUTS and len(out_shapes) == S5_PREP_OUTPUTS
    outs = pl.pallas_call(
        functools.partial(_s5_prep_kernel, n_cast=len(cast)),
        grid=(g // SSM_GB,),
        in_specs=[spec(a) for a in args] + cast_specs,
        out_specs=[spec(s) for s in out_shapes] + cast_specs,
        out_shape=out_shapes + [jax.ShapeDtypeStruct(w.shape, BF16) for w in cast],
        compiler_params=_params(("parallel",)),
        name="s5_prep",
    )(*args, *cast)
    wmat, mmat, vmat, dec_a, dec_b = outs[:S5_PREP_OUTPUTS]
    mats = (wmat, mmat, vmat, dec_a.reshape(g, SSM_SW), dec_b.reshape(g, SSM_SW))
    return mats, outs[S5_PREP_OUTPUTS:]


def _chunk_permutation():
    n = SSM_GB * LANES
    t, g, c = np.meshgrid(np.arange(SSM_GB), np.arange(SSM_GB), np.arange(SSM_GROUP),
                          indexing="ij")
    p = np.zeros((n, n), np.float32)
    p[(t * LANES + g * SSM_GROUP + c).ravel(), (g * LANES + t * SSM_GROUP + c).ravel()] = 1.0
    return jnp.asarray(p, dtype=BF16)


def _s5_kernel(u_ref, p_ref, w_ref, m_ref, v_ref, da_ref, db_ref, *rest, n_chunks, n_cast):
    y_ref = rest[n_cast]
    cat_ref, uc_ref, wx_ref, ws_ref, xp_ref = rest[2 * n_cast + 1:]
    _side_cast(rest[:n_cast], rest[n_cast + 1:2 * n_cast + 1])
    halves = SSM_T // SSM_GB
    half_w = SSM_GB * LANES

    for t in range(SSM_T):
        cat_ref[:, t * LANES:(t + 1) * LANES] = (
            u_ref[pl.ds(t, n_chunks, stride=SSM_T), :].astype(BF16))
    for half in range(halves):
        z = jnp.dot(cat_ref[:, half * half_w:(half + 1) * half_w], p_ref[...],
                    preferred_element_type=F32).astype(BF16)
        for g in range(SSM_GB):
            uc_ref[g, :, half * LANES:(half + 1) * LANES] = z[:, g * LANES:(g + 1) * LANES]

    for g in range(SSM_GB):
        w = jnp.dot(uc_ref[g], w_ref[g], preferred_element_type=F32)
        wx_ref[pl.ds(g, n_chunks, stride=SSM_GB), :] = w[:, :SSM_SW]
        ws_ref[pl.ds(g, n_chunks, stride=SSM_GB), :] = w[:, SSM_SW:]

    dec_a = da_ref[...]
    dec_b = db_ref[...]

    dec2_a = dec_a * dec_a - dec_b * dec_b
    dec2_b = 2.0 * dec_a * dec_b

    def step(n, carry):
        x, xs = carry
        r0 = pl.ds(pl.multiple_of(2 * n * SSM_GB, SSM_GB), SSM_GB)
        r1 = pl.ds(pl.multiple_of((2 * n + 1) * SSM_GB, SSM_GB), SSM_GB)
        w0, w0s = wx_ref[r0, :], ws_ref[r0, :]
        c = dec_a * w0 + dec_b * w0s + wx_ref[r1, :]
        cs = dec_a * w0s - dec_b * w0 + ws_ref[r1, :]
        xp_ref[r0, :] = x
        xp_ref[r1, :] = dec_a * x + dec_b * xs + w0
        return dec2_a * x + dec2_b * xs + c, dec2_a * xs - dec2_b * x + cs

    zero = jnp.zeros((SSM_GB, SSM_SW), F32)
    lax.fori_loop(0, n_chunks // 2, step, (zero, zero), unroll=4)

    for g in range(SSM_GB):
        y = jnp.dot(uc_ref[g], m_ref[g], preferred_element_type=F32)
        xp = xp_ref[pl.ds(g, n_chunks, stride=SSM_GB), :].astype(BF16)
        y = (y + jnp.dot(xp, v_ref[g], preferred_element_type=F32)).astype(BF16)
        for half in range(halves):
            cat_ref[:, half * half_w + g * LANES:half * half_w + (g + 1) * LANES] = (
                y[:, half * LANES:(half + 1) * LANES])

    for half in range(halves):
        z = jnp.dot(cat_ref[:, half * half_w:(half + 1) * half_w], p_ref[...],
                    preferred_element_type=F32)
        for t in range(SSM_GB):
            y_ref[pl.ds(half * SSM_GB + t, n_chunks, stride=SSM_T), :] = (
                z[:, t * LANES:(t + 1) * LANES])


def _s5(u_slabs, mats, batch, seq, cast=()):
    wmat, mmat, vmat, dec_a, dec_b = mats
    m = u_slabs.shape[1]
    n_chunks = seq // SSM_T
    perm = _chunk_permutation()
    cast_specs = _side_cast_specs(cast, SSM_SLABS * batch, lambda k, b: k * batch + b)

    def grp(shape):
        return pl.BlockSpec((SSM_GB,) + shape, lambda k, b: (k,) + (0,) * len(shape))

    seq_spec = pl.BlockSpec((None, seq, LANES), lambda k, b: (k, b, 0))
    outs = pl.pallas_call(
        functools.partial(_s5_kernel, n_chunks=n_chunks, n_cast=len(cast)),
        grid=(SSM_SLABS, batch),
        in_specs=[
            seq_spec,
            pl.BlockSpec(perm.shape, lambda k, b: (0, 0)),
            grp((SSM_CW, 2 * SSM_SW)),
            grp((SSM_CW, SSM_CW)),
            grp((SSM_SW, SSM_CW)),
            grp((SSM_SW,)),
            grp((SSM_SW,)),
        ] + cast_specs,
        out_specs=[seq_spec] + cast_specs,
        out_shape=[jax.ShapeDtypeStruct((SSM_SLABS, m, LANES), F32)]
        + [jax.ShapeDtypeStruct(w.shape, BF16) for w in cast],
        scratch_shapes=[
            pltpu.VMEM((n_chunks, SSM_T * LANES), BF16),
            pltpu.VMEM((SSM_GB, n_chunks, SSM_CW), BF16),
            pltpu.VMEM((SSM_GB * n_chunks, SSM_SW), F32),
            pltpu.VMEM((SSM_GB * n_chunks, SSM_SW), F32),
            pltpu.VMEM((SSM_GB * n_chunks, SSM_SW), F32),
        ],
        compiler_params=_params(("parallel", "parallel")),
        name="s5_chunks",
    )(u_slabs, perm, wmat, mmat, vmat, dec_a, dec_b, *cast)
    return outs[0], outs[1:]


def _outproj_kernel(x_ref, yr_ref, ys_ref, wglu_ref, bglu_ref, og_ref, wout_ref, *rest, n_cast):
    o_ref = rest[n_cast]
    _side_cast(rest[:n_cast], rest[n_cast + 1:])
    y1 =jax.nn.gelu(jnp.concatenate([ys_ref[k] for k in range(SSM_SLABS)], axis=-1))
    z = jnp.dot(y1.astype(BF16), wglu_ref[...], preferred_element_type=F32) + bglu_ref[...]
    y2 = y1 * jax.nn.sigmoid(z)
    y_ssm = _rms(y2, og_ref[...]).astype(BF16)
    acc = jnp.dot(yr_ref[...], wout_ref[:RET_WIDTH, :], preferred_element_type=F32)
    acc = acc + jnp.dot(y_ssm, wout_ref[RET_WIDTH:, :], preferred_element_type=F32)
    o_ref[...] = x_ref[...] + acc


def _outproj(x2d, y_ret, y_s5, w_glu, b_glu, out_g, w_out, cast=()):
    m = x2d.shape[0]
    cast_specs = _side_cast_specs(cast, m // ROW_TILE, lambda i: i)

    def rows(width):
        return pl.BlockSpec((ROW_TILE, width), lambda i: (i, 0))

    def whole(shape):
        return pl.BlockSpec(shape, lambda i: (0, 0))

    outs = pl.pallas_call(
        functools.partial(_outproj_kernel, n_cast=len(cast)),
        grid=(m // ROW_TILE,),
        in_specs=[
            rows(D_MODEL), rows(RET_WIDTH),
            pl.BlockSpec((SSM_SLABS, ROW_TILE, LANES), lambda i: (0, i, 0)),
            whole((SSM_WIDTH, SSM_WIDTH)), whole((1, SSM_WIDTH)), whole((1, SSM_WIDTH)),
            whole((D_MODEL, D_MODEL)),
        ] + cast_specs,
        out_specs=[rows(D_MODEL)] + cast_specs,
        out_shape=[jax.ShapeDtypeStruct((m, D_MODEL), F32)]
        + [jax.ShapeDtypeStruct(w.shape, BF16) for w in cast],
        compiler_params=_params(("parallel",)),
        name="outproj",
    )(x2d, y_ret, y_s5, w_glu, b_glu, out_g, w_out, *cast)
    return outs[0], outs[1:]


def _ffn_kernel(x_ref, g_ref, wg_ref, wu_ref, wd_ref, gf_ref, o_ref, h_ref):
    j = pl.program_id(1)
    last = pl.num_programs(1) - 1

    def step(first, final):
        if first:
            x = x_ref[...]
            h = _rms(x, g_ref[...]).astype(BF16)
            h_ref[...] = h
            base = x
        else:
            h = h_ref[...]
            base = o_ref[...]
        o = base
        acts = []
        for s in range(FF_SPLIT):
            cols = slice(s * (FF_TILE // FF_SPLIT), (s + 1) * (FF_TILE // FF_SPLIT))
            gate = jnp.dot(h, wg_ref[:, cols], preferred_element_type=F32)
            up = jnp.dot(h, wu_ref[:, cols], preferred_element_type=F32)
            acts.append((cols, (jax.nn.silu(gate) * up).astype(BF16)))
        for cols, act in acts:
            o = o + jnp.dot(act, wd_ref[cols, :], preferred_element_type=F32)
        o_ref[...] = _rms(o, gf_ref[...]) if final else o

    pl.when(j == 0)(lambda: step(True, False))
    pl.when((j > 0) & (j < last))(lambda: step(False, False))
    pl.when(j == last)(lambda: step(False, True))


def _ffn(x2d, gain, w_gate, w_up, w_down, gain_final):
    m = x2d.shape[0]
    return pl.pallas_call(
        _ffn_kernel,
        grid=(m // FFN_ROW_TILE, D_FF // FF_TILE),
        in_specs=[
            pl.BlockSpec((FFN_ROW_TILE, D_MODEL), lambda i, j: (i, 0)),
            pl.BlockSpec((1, D_MODEL), lambda i, j: (0, 0)),
            pl.BlockSpec((D_MODEL, FF_TILE), lambda i, j: (0, j)),
            pl.BlockSpec((D_MODEL, FF_TILE), lambda i, j: (0, j)),
            pl.BlockSpec((FF_TILE, D_MODEL), lambda i, j: (j, 0)),
            pl.BlockSpec((1, D_MODEL), lambda i, j: (0, 0)),
        ],
        out_specs=pl.BlockSpec((FFN_ROW_TILE, D_MODEL), lambda i, j: (i, 0)),
        out_shape=jax.ShapeDtypeStruct((m, D_MODEL), F32),
        scratch_shapes=[pltpu.VMEM((FFN_ROW_TILE, D_MODEL), BF16)],
        compiler_params=_params(("parallel", "arbitrary")),
        name="ffn",
    )(x2d, gain, w_gate, w_up, w_down, gain_final)


def _rope_tables(seq):
    half = RET_HEAD_DIM // 2
    pos = np.arange(seq, dtype=np.float64)
    freqs = ROPE_BASE ** (-np.arange(half, dtype=np.float64) / half)
    ang = pos[:, None] * freqs[None, :]
    cos = np.concatenate([np.cos(ang), np.cos(ang)], axis=-1)
    sin = np.concatenate([-np.sin(ang), np.sin(ang)], axis=-1)
    scale = np.array([1.0, RET_HEAD_DIM ** -0.5])[:, None, None]
    return (jnp.asarray(cos[None] * scale, dtype=F32), jnp.asarray(sin[None] * scale, dtype=F32))


def kernel(x, norm_mix_g, w_in, ret_gn_g, ssm_a_re, ssm_a_im, ssm_log_dt, ssm_b_re, ssm_b_im,
           ssm_c_re, ssm_c_im, ssm_d, ssm_w_glu, ssm_b_glu, ssm_out_g, w_out, norm_ffn_g,
           w_gate, w_up, w_down, norm_final_g):
    batch, seq, d = x.shape
    depth = w_in.shape[0]
    assert d == D_MODEL and seq % ROW_TILE == 0 and seq % (RET_BLOCK * RET_STEP_BLOCKS) == 0 and seq % SSM_T == 0
    m = batch * seq
    rope = _rope_tables(seq)
    x2d = x.reshape(m, d)

    for l in range(depth):
        mats, (w_in_b,) = _s5_prep(ssm_a_re[l], ssm_a_im[l], ssm_log_dt[l], ssm_b_re[l],
                                   ssm_b_im[l], ssm_c_re[l], ssm_c_im[l], ssm_d[l],
                                   cast=(w_in[l],))
        (qk, vg, u_slabs), (w_gate_b,) = _inproj(x2d, norm_mix_g[l][None], w_in_b, rope, seq,
                                                 cast=(w_gate[l],))
        y_ret, (w_out_b, w_glu_b) = _retention(qk, vg, ret_gn_g[l][None], batch, seq,
                                               cast=(w_out[l], ssm_w_glu[l]))
        y_s5, (w_down_b,) = _s5(u_slabs, mats, batch, seq, cast=(w_down[l],))

        x2d, (w_up_b,) = _outproj(x2d, y_ret, y_s5, w_glu_b, ssm_b_glu[l][None],
                                  ssm_out_g[l][None], w_out_b, cast=(w_up[l],))
        last = l == depth - 1
        assert last, "fused final norm assumes a single layer"
        x2d = _ffn(x2d, norm_ffn_g[l][None], w_gate_b, w_up_b, w_down_b, norm_final_g[None])
    return x2d.reshape(batch, seq, d)
```

```python
import functools
import math

import jax
import jax.numpy as jnp
import numpy as np
from jax import lax
from jax.experimental import pallas as pl
from jax.experimental.pallas import tpu as pltpu

D_MODEL = 2048
CHUNK = 64
RET_WIDTH = D_MODEL // 2
RET_HEADS = 8
RET_HEAD_DIM = RET_WIDTH // RET_HEADS
SSM_WIDTH = D_MODEL - RET_WIDTH
SSM_GROUP = 16
SSM_GROUPS = SSM_WIDTH // SSM_GROUP
SSM_STATE = 64
D_FF = -(-8 * D_MODEL // (3 * 256)) * 256
IN_WIDTH = 4 * RET_WIDTH + SSM_WIDTH
ROPE_BASE = 10000.0
EPS = 1e-6

F32 = jnp.float32
BF16 = jnp.bfloat16

V7X_VMEM_BYTES = 64 * 1024 * 1024
VMEM_LIMIT = V7X_VMEM_BYTES - 4 * 1024 * 1024

ROW_TILE = 512
IN_COL_TILE = RET_WIDTH
FFN_ROW_TILE = 1024
FF_TILE = 512
FF_SPLIT = 2
RET_BLOCK = 256
SSM_T = 16
SSM_CW = SSM_T * SSM_GROUP
LANES = 128
SUBLANES = 8
BF16_ROWS = 16
SSM_GB = LANES // SSM_GROUP
SSM_SLABS = SSM_GROUPS // SSM_GB
SSM_SW = 2 * SSM_STATE


def _params(sem):
    return pltpu.CompilerParams(dimension_semantics=sem, vmem_limit_bytes=VMEM_LIMIT)


def _rms(x, g):
    return x * lax.rsqrt(jnp.mean(x * x, axis=-1, keepdims=True) + EPS) * g


def _side_cast_specs(weights, n_steps, step_of):
    specs = []
    for w in weights:
        rows, cols = w.shape
        assert rows % (n_steps * BF16_ROWS) == 0, (w.shape, n_steps)
        specs.append(pl.BlockSpec((rows // n_steps, cols),
                                  lambda *idx, step_of=step_of: (step_of(*idx), 0)))
    return specs


def _side_cast(srcs, dsts):
    for src, dst in zip(srcs, dsts):
        dst[...] = src[...].astype(BF16)


def _retention_unit(qkvg_ref, rows, head, dmat_ref, qdec_ref, kdec_ref, gn_ref, st_ref, y_ref,
                    block_decay):
    col = head * RET_HEAD_DIM
    hs = slice(col, col + RET_HEAD_DIM)
    q = qkvg_ref[rows, col:col + RET_HEAD_DIM]
    k = qkvg_ref[rows, RET_WIDTH + col:RET_WIDTH + col + RET_HEAD_DIM]
    v = qkvg_ref[rows, 2 * RET_WIDTH + col:2 * RET_WIDTH + col + RET_HEAD_DIM]
    s = lax.dot_general(q, k, (((1,), (1,)), ((), ())),
                        preferred_element_type=F32) * dmat_ref[head]
    out = jnp.dot(s.astype(BF16), v, preferred_element_type=F32)
    state = st_ref[head]
    out = out + jnp.dot(q, state.astype(BF16), preferred_element_type=F32) * qdec_ref[head]
    kv = lax.dot_general(k * kdec_ref[head].astype(BF16), v, (((0,), (0,)), ((), ())),
                         preferred_element_type=F32)
    st_ref[head] = state * block_decay[head] + kv
    mu = jnp.mean(out, axis=-1, keepdims=True)
    cen = out - mu
    var = jnp.mean(cen * cen, axis=-1, keepdims=True)
    normed = cen * lax.rsqrt(var + EPS) * gn_ref[:, hs]
    gate = qkvg_ref[rows, 3 * RET_WIDTH + col:3 * RET_WIDTH + col + RET_HEAD_DIM].astype(F32)
    y_ref[rows, hs] = (jax.nn.silu(gate) * normed).astype(BF16)


def _mix_in_kernel(x_ref, g_ref, w_ref, cos_ref, sin_ref, dmat_ref, qdec_ref, kdec_ref, gn_ref,
                   u_ref, y_ref, qkvg_a, qkvg_b, st_ref, *, n_tiles, tiles_per_seq, block_decay):
    i = pl.program_id(0)

    @pl.when(i == 0)
    def _():
        qkvg_b[...] = jnp.zeros_like(qkvg_b)
        st_ref[...] = jnp.zeros_like(st_ref)

    @pl.when((i + tiles_per_seq - 1) % tiles_per_seq == 0)
    def _():
        st_ref[...] = jnp.zeros_like(st_ref)

    def retention_units(prev_ref):
        for sub in range(ROW_TILE // RET_BLOCK):
            rows = slice(sub * RET_BLOCK, (sub + 1) * RET_BLOCK)
            for head in range(RET_HEADS):
                yield functools.partial(_retention_unit, prev_ref, rows, head, dmat_ref, qdec_ref,
                                        kdec_ref, gn_ref, st_ref, y_ref, block_decay)

    def step(cur_ref, prev_ref, project):
        units = list(retention_units(prev_ref))
        if not project:
            for unit in units:
                unit()
            return
        x = x_ref[...]
        inv = lax.rsqrt(jnp.mean(x * x, axis=-1, keepdims=True) + EPS)
        h = (x * g_ref[...]).astype(BF16)
        n_blocks = IN_WIDTH // IN_COL_TILE
        per_block = -(-len(units) // n_blocks)
        for c in range(n_blocks):
            acc = jnp.dot(h, w_ref[:, c * IN_COL_TILE:(c + 1) * IN_COL_TILE],
                          preferred_element_type=F32)
            if c < 2:
                cos = cos_ref[c] * inv
                sin = sin_ref[c] * inv
                for h_i in range(RET_HEADS):
                    lo = c * RET_WIDTH + h_i * RET_HEAD_DIM
                    a = acc[:, h_i * RET_HEAD_DIM:(h_i + 1) * RET_HEAD_DIM]
                    cur_ref[:, lo:lo + RET_HEAD_DIM] = (
                        a * cos + pltpu.roll(a, RET_HEAD_DIM // 2, axis=1) * sin).astype(BF16)
            elif c < 4:
                cur_ref[:, c * RET_WIDTH:(c + 1) * RET_WIDTH] = (acc * inv).astype(BF16)
            else:
                acc = acc * inv
                for k in range(SSM_SLABS):
                    u_ref[k] = acc[:, k * LANES:(k + 1) * LANES]
            for unit in units[c * per_block:(c + 1) * per_block]:
                unit()

    even = i % 2 == 0
    pl.when((i < n_tiles) & even)(lambda: step(qkvg_a, qkvg_b, True))
    pl.when((i < n_tiles) & jnp.logical_not(even))(lambda: step(qkvg_b, qkvg_a, True))
    last_prev = qkvg_b if n_tiles % 2 == 0 else qkvg_a
    pl.when(i == n_tiles)(lambda: step(None, last_prev, False))


def _retention_log_decay():
    return np.log1p(-(2.0 ** (-5.0 - np.arange(RET_HEADS, dtype=np.float64))))


def _retention_tables():
    log_g = _retention_log_decay()
    idx = np.arange(RET_BLOCK, dtype=np.float64)
    chunk = np.arange(RET_BLOCK) // CHUNK
    diff = idx[:, None] - idx[None, :]
    same = chunk[:, None] == chunk[None, :]
    earlier = chunk[None, :] < chunk[:, None]
    dist = np.where(same, np.abs(diff), diff)
    dmat = np.where((same | earlier)[None], np.exp(log_g[:, None, None] * dist[None]), 0.0)
    qdec = np.exp(log_g[:, None] * (idx + 1.0)[None, :])
    kdec = np.exp(log_g[:, None] * (RET_BLOCK - 1.0 - idx)[None, :])
    qdec = np.broadcast_to(qdec[:, :, None], (RET_HEADS, RET_BLOCK, RET_HEAD_DIM))
    kdec = np.broadcast_to(kdec[:, :, None], (RET_HEADS, RET_BLOCK, RET_HEAD_DIM))
    return tuple(jnp.asarray(a, dtype=F32) for a in (dmat, qdec, kdec))


def _mix_in(x2d, gain, w_bf16, rope, gn_gain, seq):
    m = x2d.shape[0]
    assert IN_COL_TILE == RET_WIDTH == SSM_WIDTH and ROW_TILE % RET_BLOCK == 0
    n_tiles = m // ROW_TILE
    tiles_per_seq = seq // ROW_TILE
    dmat, qdec, kdec = _retention_tables()
    block_decay = tuple(float(v) for v in np.exp(_retention_log_decay() * RET_BLOCK))
    cur = lambda i: jnp.minimum(i, n_tiles - 1)

    def whole(shape):
        return pl.BlockSpec(shape, lambda i: (0,) * len(shape))

    tab = pl.BlockSpec((2, ROW_TILE, RET_HEAD_DIM), lambda i: (0, cur(i) % tiles_per_seq, 0))
    return pl.pallas_call(
        functools.partial(_mix_in_kernel, n_tiles=n_tiles, tiles_per_seq=tiles_per_seq,
                          block_decay=block_decay),
        grid=(n_tiles + 1,),
        in_specs=[
            pl.BlockSpec((ROW_TILE, D_MODEL), lambda i: (cur(i), 0)),
            whole((1, D_MODEL)),
            pl.BlockSpec((D_MODEL, IN_WIDTH), lambda i: (0, 0), pipeline_mode=pl.Buffered(1)),
            tab, tab,
            whole((RET_HEADS, RET_BLOCK, RET_BLOCK)),
            whole((RET_HEADS, RET_BLOCK, RET_HEAD_DIM)),
            whole((RET_HEADS, RET_BLOCK, RET_HEAD_DIM)),
            whole((1, RET_WIDTH)),
        ],
        out_specs=[
            pl.BlockSpec((SSM_SLABS, ROW_TILE, LANES), lambda i: (0, cur(i), 0)),
            pl.BlockSpec((ROW_TILE, RET_WIDTH), lambda i: (jnp.maximum(i - 1, 0), 0)),
        ],
        out_shape=[jax.ShapeDtypeStruct((SSM_SLABS, m, LANES), F32),
                   jax.ShapeDtypeStruct((m, RET_WIDTH), BF16)],
        scratch_shapes=[pltpu.VMEM((ROW_TILE, 4 * RET_WIDTH), BF16),
                        pltpu.VMEM((ROW_TILE, 4 * RET_WIDTH), BF16),
                        pltpu.VMEM((RET_HEADS, RET_HEAD_DIM, RET_HEAD_DIM), F32)],
        compiler_params=_params(("arbitrary",)),
        name="mix_in",
    )(x2d, gain, w_bf16, *rope, dmat, qdec, kdec, gn_gain)


def _cmul(ar, ai, br, bi):
    return ar * br - ai * bi, ar * bi + ai * br


def _zoh(a_re, a_im, dt):
    e = jnp.exp(a_re * dt)
    lr = e * jnp.cos(a_im * dt)
    li = e * jnp.sin(a_im * dt)
    inv = 1.0 / (a_re * a_re + a_im * a_im)
    xr = lr - 1.0
    return lr, li, (xr * a_re + li * a_im) * inv, (li * a_re - xr * a_im) * inv


def _powers_by_bits(expo, lr, li, n_bits):
    pr = jnp.ones(expo.shape, F32)
    pi = jnp.zeros(expo.shape, F32)
    qr, qi = lr, li
    for bit in range(n_bits):
        sel = (expo & (1 << bit)) != 0
        mr, mi = _cmul(pr, pi, qr, qi)
        pr = jnp.where(sel, mr, pr)
        pi = jnp.where(sel, mi, pi)
        qr, qi = _cmul(qr, qi, qr, qi)
    return pr, pi, qr, qi


S5_PREP_INPUTS = 10
S5_PREP_OUTPUTS = 5


def _select_t(a, sel):
    return lax.dot_general(a, sel, (((0,), (0,)), ((), ())), preferred_element_type=F32,
                           precision=lax.Precision.HIGHEST)


def _s5_prep_kernel(*refs, n_cast):
    ins = refs[:S5_PREP_INPUTS]
    outs = refs[S5_PREP_INPUTS + n_cast:S5_PREP_INPUTS + n_cast + S5_PREP_OUTPUTS]
    _side_cast(refs[S5_PREP_INPUTS:S5_PREP_INPUTS + n_cast],
               refs[S5_PREP_INPUTS + n_cast + S5_PREP_OUTPUTS:])
    for g in range(SSM_GB):
        _s5_prep_group(*[r.at[g] for r in ins + outs])


def _s5_prep_group(logdt_ref, are_l_ref, aim_l_ref, are_s_ref, aim_s_ref, bre_ref, bim_ref,
                   cre_ref, cim_ref, d_ref, w_ref, m_ref, v_ref, da_ref, db_ref):
    n_bits = SSM_T.bit_length() - 1
    dt = jnp.exp(logdt_ref[...])

    lr, li, br, bi = _zoh(are_l_ref[...], aim_l_ref[...], dt)
    dup = (lax.broadcasted_iota(jnp.int32, (SSM_STATE, LANES), 1) % SSM_STATE
           == lax.broadcasted_iota(jnp.int32, (SSM_STATE, LANES), 0)).astype(F32)
    bbr, bbi = _cmul(br, bi, _select_t(bre_ref[...], dup), _select_t(bim_ref[...], dup))
    low = lax.broadcasted_iota(jnp.int32, (SSM_GROUP, LANES), 1) < SSM_STATE
    x1 = jnp.where(low, bbr, bbi)
    x2 = jnp.where(low, -bbi, bbr)
    srow = lax.broadcasted_iota(jnp.int32, (SSM_T, LANES), 0)
    pr, pi, lr_t, li_t = _powers_by_bits(SSM_T - 1 - srow, lr, li, n_bits)
    w = jnp.concatenate([pr[s:s + 1] * x1 + pi[s:s + 1] * x2 for s in range(SSM_T)], axis=0)
    w_ref[:, :SSM_SW] = w.astype(BF16)
    w_ref[:, SSM_SW:] = pltpu.roll(w, SSM_STATE, axis=1).astype(BF16)
    da_ref[...] = lr_t
    db_ref[...] = jnp.where(low[:1], -li_t, li_t)

    lrs, lis, _, _ = _zoh(are_s_ref[...], aim_s_ref[...], dt)
    tau = lax.broadcasted_iota(jnp.int32, (SSM_STATE, SSM_CW), 1) // SSM_GROUP
    p0r, p0i, _, _ = _powers_by_bits(tau, lrs, lis, n_bits)
    tile = (lax.broadcasted_iota(jnp.int32, (SSM_GROUP, SSM_CW), 1) % SSM_GROUP
            == lax.broadcasted_iota(jnp.int32, (SSM_GROUP, SSM_CW), 0)).astype(F32)
    cr, ci = _select_t(cre_ref[...], tile), _select_t(cim_ref[...], tile)
    gr, gi = _cmul(cr, ci, p0r, p0i)
    lhs = jnp.where(low, bbr, -bbi)
    k_all = jnp.dot(lhs, jnp.concatenate([gr, gi], axis=0),
                    preferred_element_type=F32, precision=lax.Precision.HIGHEST)
    crow = lax.broadcasted_iota(jnp.int32, (SSM_GROUP, SSM_CW), 0)
    clane = lax.broadcasted_iota(jnp.int32, (SSM_GROUP, SSM_CW), 1)
    k_all = k_all + jnp.where(crow == clane, d_ref[...], 0.0)
    for s in range(SSM_T):
        shifted = k_all if s == 0 else pltpu.roll(k_all, s * SSM_GROUP, axis=1)
        m_ref[s * SSM_GROUP:(s + 1) * SSM_GROUP, :] = jnp.where(
            clane >= s * SSM_GROUP, shifted, 0.0).astype(BF16)
    p1r, p1i = _cmul(p0r, p0i, lrs, lis)
    vr, vi = _cmul(cr, ci, p1r, p1i)
    v_ref[:SSM_STATE, :] = vr.astype(BF16)
    v_ref[SSM_STATE:, :] = (-vi).astype(BF16)


def _s5_prep(a_re, a_im, log_dt, b_re, b_im, c_re, c_im, d_skip, cast=()):
    g = SSM_GROUPS
    cast_specs = _side_cast_specs(cast, g // SSM_GB, lambda i: i)
    dup = lambda a: jnp.concatenate([a, a], axis=-1)
    args = (
        log_dt.reshape(g, 1, 1),
        dup(a_re)[:, None, :], dup(a_im)[:, None, :],
        a_re[:, :, None], a_im[:, :, None],
        b_re, b_im,
        c_re, c_im,
        d_skip.reshape(g, SSM_GROUP, 1),
    )

    def spec(a):
        return pl.BlockSpec((SSM_GB,) + a.shape[1:], lambda i: (i, 0, 0))

    out_shapes = [
        jax.ShapeDtypeStruct((g, SSM_CW, 2 * SSM_SW), BF16),
        jax.ShapeDtypeStruct((g, SSM_CW, SSM_CW), BF16),
        jax.ShapeDtypeStruct((g, SSM_SW, SSM_CW), BF16),
        jax.ShapeDtypeStruct((g, 1, SSM_SW), F32),
        jax.ShapeDtypeStruct((g, 1, SSM_SW), F32),
    ]
    assert len(args) == S5_PREP_INPUTS and len(out_shapes) == S5_PREP_OUTPUTS
    outs = pl.pallas_call(
        functools.partial(_s5_prep_kernel, n_cast=len(cast)),
        grid=(g // SSM_GB,),
        in_specs=[spec(a) for a in args] + cast_specs,
        out_specs=[spec(s) for s in out_shapes] + cast_specs,
        out_shape=out_shapes + [jax.ShapeDtypeStruct(w.shape, BF16) for w in cast],
        compiler_params=_params(("parallel",)),
        name="s5_prep",
    )(*args, *cast)
    wmat, mmat, vmat, dec_a, dec_b = outs[:S5_PREP_OUTPUTS]
    mats = (wmat, mmat, vmat, dec_a.reshape(g, SSM_SW), dec_b.reshape(g, SSM_SW))
    return mats, outs[S5_PREP_OUTPUTS:]


def _chunk_permutation():
    n = SSM_GB * LANES
    t, g, c = np.meshgrid(np.arange(SSM_GB), np.arange(SSM_GB), np.arange(SSM_GROUP),
                          indexing="ij")
    p = np.zeros((n, n), np.float32)
    p[(t * LANES + g * SSM_GROUP + c).ravel(), (g * LANES + t * SSM_GROUP + c).ravel()] = 1.0
    return jnp.asarray(p, dtype=BF16)


def _s5_kernel(u_ref, p_ref, w_ref, m_ref, v_ref, da_ref, db_ref, *rest, n_chunks, n_cast):
    y_ref = rest[n_cast]
    cat_ref, uc_ref, wx_ref, ws_ref, xp_ref = rest[2 * n_cast + 1:]
    _side_cast(rest[:n_cast], rest[n_cast + 1:2 * n_cast + 1])
    halves = SSM_T // SSM_GB
    half_w = SSM_GB * LANES

    for t in range(SSM_T):
        cat_ref[:, t * LANES:(t + 1) * LANES] = (
            u_ref[pl.ds(t, n_chunks, stride=SSM_T), :].astype(BF16))
    for half in range(halves):
        z = jnp.dot(cat_ref[:, half * half_w:(half + 1) * half_w], p_ref[...],
                    preferred_element_type=F32).astype(BF16)
        for g in range(SSM_GB):
            uc_ref[g, :, half * LANES:(half + 1) * LANES] = z[:, g * LANES:(g + 1) * LANES]

    for g in range(SSM_GB):
        w = jnp.dot(uc_ref[g], w_ref[g], preferred_element_type=F32)
        wx_ref[pl.ds(g, n_chunks, stride=SSM_GB), :] = w[:, :SSM_SW]
        ws_ref[pl.ds(g, n_chunks, stride=SSM_GB), :] = w[:, SSM_SW:]

    dec_a = da_ref[...]
    dec_b = db_ref[...]

    dec2_a = dec_a * dec_a - dec_b * dec_b
    dec2_b = 2.0 * dec_a * dec_b

    def step(n, carry):
        x, xs = carry
        r0 = pl.ds(pl.multiple_of(2 * n * SSM_GB, SSM_GB), SSM_GB)
        r1 = pl.ds(pl.multiple_of((2 * n + 1) * SSM_GB, SSM_GB), SSM_GB)
        w0, w0s = wx_ref[r0, :], ws_ref[r0, :]
        c = dec_a * w0 + dec_b * w0s + wx_ref[r1, :]
        cs = dec_a * w0s - dec_b * w0 + ws_ref[r1, :]
        xp_ref[r0, :] = x
        xp_ref[r1, :] = dec_a * x + dec_b * xs + w0
        return dec2_a * x + dec2_b * xs + c, dec2_a * xs - dec2_b * x + cs

    zero = jnp.zeros((SSM_GB, SSM_SW), F32)
    lax.fori_loop(0, n_chunks // 2, step, (zero, zero), unroll=4)

    for g in range(SSM_GB):
        y = jnp.dot(uc_ref[g], m_ref[g], preferred_element_type=F32)
        xp = xp_ref[pl.ds(g, n_chunks, stride=SSM_GB), :].astype(BF16)
        y = (y + jnp.dot(xp, v_ref[g], preferred_element_type=F32)).astype(BF16)
        for half in range(halves):
            cat_ref[:, half * half_w + g * LANES:half * half_w + (g + 1) * LANES] = (
                y[:, half * LANES:(half + 1) * LANES])

    for half in range(halves):
        z = jnp.dot(cat_ref[:, half * half_w:(half + 1) * half_w], p_ref[...],
                    preferred_element_type=F32)
        for t in range(SSM_GB):
            y_ref[pl.ds(half * SSM_GB + t, n_chunks, stride=SSM_T), :] = (
                z[:, t * LANES:(t + 1) * LANES])


def _s5(u_slabs, mats, batch, seq, cast=()):
    wmat, mmat, vmat, dec_a, dec_b = mats
    m = u_slabs.shape[1]
    n_chunks = seq // SSM_T
    perm = _chunk_permutation()
    cast_specs = _side_cast_specs(cast, SSM_SLABS * batch, lambda k, b: k * batch + b)

    def grp(shape):
        return pl.BlockSpec((SSM_GB,) + shape, lambda k, b: (k,) + (0,) * len(shape))

    seq_spec = pl.BlockSpec((None, seq, LANES), lambda k, b: (k, b, 0))
    outs = pl.pallas_call(
        functools.partial(_s5_kernel, n_chunks=n_chunks, n_cast=len(cast)),
        grid=(SSM_SLABS, batch),
        in_specs=[
            seq_spec,
            pl.BlockSpec(perm.shape, lambda k, b: (0, 0)),
            grp((SSM_CW, 2 * SSM_SW)),
            grp((SSM_CW, SSM_CW)),
            grp((SSM_SW, SSM_CW)),
            grp((SSM_SW,)),
            grp((SSM_SW,)),
        ] + cast_specs,
        out_specs=[seq_spec] + cast_specs,
        out_shape=[jax.ShapeDtypeStruct((SSM_SLABS, m, LANES), F32)]
        + [jax.ShapeDtypeStruct(w.shape, BF16) for w in cast],
        scratch_shapes=[
            pltpu.VMEM((n_chunks, SSM_T * LANES), BF16),
            pltpu.VMEM((SSM_GB, n_chunks, SSM_CW), BF16),
            pltpu.VMEM((SSM_GB * n_chunks, SSM_SW), F32),
            pltpu.VMEM((SSM_GB * n_chunks, SSM_SW), F32),
            pltpu.VMEM((SSM_GB * n_chunks, SSM_SW), F32),
        ],
        compiler_params=_params(("parallel", "parallel")),
        name="s5_chunks",
    )(u_slabs, perm, wmat, mmat, vmat, dec_a, dec_b, *cast)
    return outs[0], outs[1:]


def _outproj_kernel(x_ref, yr_ref, ys_ref, wglu_ref, bglu_ref, og_ref, wout_ref, *rest, n_cast):
    o_ref = rest[n_cast]
    _side_cast(rest[:n_cast], rest[n_cast + 1:])
    half = D_MODEL // 2
    yr = yr_ref[...]
    acc_lo = jnp.dot(yr, wout_ref[:RET_WIDTH, :half], preferred_element_type=F32)
    y1 = jax.nn.gelu(jnp.concatenate([ys_ref[k] for k in range(SSM_SLABS)], axis=-1))
    z = jnp.dot(y1.astype(BF16), wglu_ref[...], preferred_element_type=F32) + bglu_ref[...]
    acc_hi = jnp.dot(yr, wout_ref[:RET_WIDTH, half:], preferred_element_type=F32)
    y2 = y1 * jax.nn.sigmoid(z)
    y_ssm = _rms(y2, og_ref[...]).astype(BF16)
    o_ref[:, :half] = x_ref[:, :half] + acc_lo + jnp.dot(
        y_ssm, wout_ref[RET_WIDTH:, :half], preferred_element_type=F32)
    o_ref[:, half:] = x_ref[:, half:] + acc_hi + jnp.dot(
        y_ssm, wout_ref[RET_WIDTH:, half:], preferred_element_type=F32)


def _outproj(x2d, y_ret, y_s5, w_glu, b_glu, out_g, w_out, cast=()):
    m = x2d.shape[0]
    cast_specs = _side_cast_specs(cast, m // ROW_TILE, lambda i: i)

    def rows(width):
        return pl.BlockSpec((ROW_TILE, width), lambda i: (i, 0))

    def whole(shape):
        return pl.BlockSpec(shape, lambda i: (0, 0))

    outs = pl.pallas_call(
        functools.partial(_outproj_kernel, n_cast=len(cast)),
        grid=(m // ROW_TILE,),
        in_specs=[
            rows(D_MODEL), rows(RET_WIDTH),
            pl.BlockSpec((SSM_SLABS, ROW_TILE, LANES), lambda i: (0, i, 0)),
            whole((SSM_WIDTH, SSM_WIDTH)), whole((1, SSM_WIDTH)), whole((1, SSM_WIDTH)),
            whole((D_MODEL, D_MODEL)),
        ] + cast_specs,
        out_specs=[rows(D_MODEL)] + cast_specs,
        out_shape=[jax.ShapeDtypeStruct((m, D_MODEL), F32)]
        + [jax.ShapeDtypeStruct(w.shape, BF16) for w in cast],
        compiler_params=_params(("parallel",)),
        name="outproj",
    )(x2d, y_ret, y_s5, w_glu, b_glu, out_g, w_out, *cast)
    return outs[0], outs[1:]


def _ffn_kernel(x_ref, g_ref, wg_ref, wu_ref, wd_ref, gf_ref, o_ref, h_ref):
    j = pl.program_id(1)
    last = pl.num_programs(1) - 1

    def step(first, final):
        if first:
            x = x_ref[...]
            h = _rms(x, g_ref[...]).astype(BF16)
            h_ref[...] = h
            base = x
        else:
            h = h_ref[...]
            base = o_ref[...]
        o = base
        acts = []
        for s in range(FF_SPLIT):
            cols = slice(s * (FF_TILE // FF_SPLIT), (s + 1) * (FF_TILE // FF_SPLIT))
            gate = jnp.dot(h, wg_ref[:, cols], preferred_element_type=F32)
            up = jnp.dot(h, wu_ref[:, cols], preferred_element_type=F32)
            acts.append((cols, (jax.nn.silu(gate) * up).astype(BF16)))
        for cols, act in acts:
            o = o + jnp.dot(act, wd_ref[cols, :], preferred_element_type=F32)
        o_ref[...] = _rms(o, gf_ref[...]) if final else o

    pl.when(j == 0)(lambda: step(True, False))
    pl.when((j > 0) & (j < last))(lambda: step(False, False))
    pl.when(j == last)(lambda: step(False, True))


def _ffn(x2d, gain, w_gate, w_up, w_down, gain_final):
    m = x2d.shape[0]
    return pl.pallas_call(
        _ffn_kernel,
        grid=(m // FFN_ROW_TILE, D_FF // FF_TILE),
        in_specs=[
            pl.BlockSpec((FFN_ROW_TILE, D_MODEL), lambda i, j: (i, 0)),
            pl.BlockSpec((1, D_MODEL), lambda i, j: (0, 0)),
            pl.BlockSpec((D_MODEL, FF_TILE), lambda i, j: (0, j)),
            pl.BlockSpec((D_MODEL, FF_TILE), lambda i, j: (0, j)),
            pl.BlockSpec((FF_TILE, D_MODEL), lambda i, j: (j, 0)),
            pl.BlockSpec((1, D_MODEL), lambda i, j: (0, 0)),
        ],
        out_specs=pl.BlockSpec((FFN_ROW_TILE, D_MODEL), lambda i, j: (i, 0)),
        out_shape=jax.ShapeDtypeStruct((m, D_MODEL), F32),
        scratch_shapes=[pltpu.VMEM((FFN_ROW_TILE, D_MODEL), BF16)],
        compiler_params=_params(("parallel", "arbitrary")),
        name="ffn",
    )(x2d, gain, w_gate, w_up, w_down, gain_final)


def _rope_tables(seq):
    half = RET_HEAD_DIM // 2
    pos = np.arange(seq, dtype=np.float64)
    freqs = ROPE_BASE ** (-np.arange(half, dtype=np.float64) / half)
    ang = pos[:, None] * freqs[None, :]
    cos = np.concatenate([np.cos(ang), np.cos(ang)], axis=-1)
    sin = np.concatenate([-np.sin(ang), np.sin(ang)], axis=-1)
    scale = np.array([1.0, RET_HEAD_DIM ** -0.5])[:, None, None]
    return (jnp.asarray(cos[None] * scale, dtype=F32), jnp.asarray(sin[None] * scale, dtype=F32))


def kernel(x, norm_mix_g, w_in, ret_gn_g, ssm_a_re, ssm_a_im, ssm_log_dt, ssm_b_re, ssm_b_im,
           ssm_c_re, ssm_c_im, ssm_d, ssm_w_glu, ssm_b_glu, ssm_out_g, w_out, norm_ffn_g,
           w_gate, w_up, w_down, norm_final_g):
    batch, seq, d = x.shape
    depth = w_in.shape[0]
    assert d == D_MODEL and seq % ROW_TILE == 0 and seq % RET_BLOCK == 0 and seq % SSM_T == 0
    m = batch * seq
    rope = _rope_tables(seq)
    x2d = x.reshape(m, d)

    for l in range(depth):
        mats, (w_in_b, w_out_b, w_glu_b) = _s5_prep(
            ssm_a_re[l], ssm_a_im[l], ssm_log_dt[l], ssm_b_re[l], ssm_b_im[l], ssm_c_re[l],
            ssm_c_im[l], ssm_d[l], cast=(w_in[l], w_out[l], ssm_w_glu[l]))
        u_slabs, y_ret = _mix_in(x2d, norm_mix_g[l][None], w_in_b, rope, ret_gn_g[l][None], seq)
        y_s5, (w_down_b, w_gate_b) = _s5(u_slabs, mats, batch, seq, cast=(w_down[l], w_gate[l]))

        x2d, (w_up_b,) = _outproj(x2d, y_ret, y_s5, w_glu_b, ssm_b_glu[l][None],
                                  ssm_out_g[l][None], w_out_b, cast=(w_up[l],))
        last = l == depth - 1
        assert last, "fused final norm assumes a single layer"
        x2d = _ffn(x2d, norm_ffn_g[l][None], w_gate_b, w_up_b, w_down_b, norm_final_g[None])
    return x2d.reshape(batch, seq, d)
```

```python
import functools
import math

import jax
import jax.numpy as jnp
import numpy as np
from jax import lax
from jax.experimental import pallas as pl
from jax.experimental.pallas import tpu as pltpu

D_MODEL = 2048
CHUNK = 64
RET_WIDTH = D_MODEL // 2
RET_HEADS = 8
RET_HEAD_DIM = RET_WIDTH // RET_HEADS
SSM_WIDTH = D_MODEL - RET_WIDTH
SSM_GROUP = 16
SSM_GROUPS = SSM_WIDTH // SSM_GROUP
SSM_STATE = 64
D_FF = -(-8 * D_MODEL // (3 * 256)) * 256
IN_WIDTH = 4 * RET_WIDTH + SSM_WIDTH
ROPE_BASE = 10000.0
EPS = 1e-6

F32 = jnp.float32
BF16 = jnp.bfloat16

V7X_VMEM_BYTES = 64 * 1024 * 1024
VMEM_LIMIT = V7X_VMEM_BYTES - 4 * 1024 * 1024

ROW_TILE = 512
IN_COL_TILE = RET_WIDTH
FFN_ROW_TILE = 1024
FF_TILE = 512
FF_SPLIT = 2
RET_BLOCK = 256
SSM_T = 16
SSM_CW = SSM_T * SSM_GROUP
LANES = 128
SUBLANES = 8
BF16_ROWS = 16
SSM_GB = LANES // SSM_GROUP
SSM_SLABS = SSM_GROUPS // SSM_GB
SSM_SW = 2 * SSM_STATE


def _params(sem):
    return pltpu.CompilerParams(dimension_semantics=sem, vmem_limit_bytes=VMEM_LIMIT)


def _rms(x, g):
    return x * lax.rsqrt(jnp.mean(x * x, axis=-1, keepdims=True) + EPS) * g


def _side_cast_specs(weights, n_steps, step_of):
    specs = []
    for w in weights:
        rows, cols = w.shape
        assert rows % (n_steps * BF16_ROWS) == 0, (w.shape, n_steps)
        specs.append(pl.BlockSpec((rows // n_steps, cols),
                                  lambda *idx, step_of=step_of: (step_of(*idx), 0)))
    return specs


def _side_cast(srcs, dsts):
    for src, dst in zip(srcs, dsts):
        dst[...] = src[...].astype(BF16)


def _retention_unit(qkvg_ref, rows, head, dmat_ref, qdec_ref, kdec_ref, gn_ref, st_ref, y_ref,
                    block_decay):
    col = head * RET_HEAD_DIM
    hs = slice(col, col + RET_HEAD_DIM)
    q = qkvg_ref[rows, col:col + RET_HEAD_DIM]
    k = qkvg_ref[rows, RET_WIDTH + col:RET_WIDTH + col + RET_HEAD_DIM]
    v = qkvg_ref[rows, 2 * RET_WIDTH + col:2 * RET_WIDTH + col + RET_HEAD_DIM]
    s = lax.dot_general(q, k, (((1,), (1,)), ((), ())),
                        preferred_element_type=F32) * dmat_ref[head]
    out = jnp.dot(s.astype(BF16), v, preferred_element_type=F32)
    state = st_ref[head]
    out = out + jnp.dot(q, state.astype(BF16), preferred_element_type=F32) * qdec_ref[head]
    kv = lax.dot_general(k * kdec_ref[head].astype(BF16), v, (((0,), (0,)), ((), ())),
                         preferred_element_type=F32)
    st_ref[head] = state * block_decay[head] + kv
    mu = jnp.mean(out, axis=-1, keepdims=True)
    cen = out - mu
    var = jnp.mean(cen * cen, axis=-1, keepdims=True)
    normed = cen * lax.rsqrt(var + EPS) * gn_ref[:, hs]
    gate = qkvg_ref[rows, 3 * RET_WIDTH + col:3 * RET_WIDTH + col + RET_HEAD_DIM].astype(F32)
    y_ref[rows, hs] = (jax.nn.silu(gate) * normed).astype(BF16)


def _mix_in_kernel(x_ref, g_ref, w_ref, cos_ref, sin_ref, dmat_ref, qdec_ref, kdec_ref, gn_ref,
                   *rest, n_tiles, tiles_per_seq, block_decay, n_cast):
    u_ref, y_ref = rest[n_cast:n_cast + 2]
    qkvg_a, qkvg_b, st_ref = rest[-3:]
    _side_cast(rest[:n_cast], rest[n_cast + 2:-3])
    i = pl.program_id(0)

    @pl.when(i == 0)
    def _():
        qkvg_b[...] = jnp.zeros_like(qkvg_b)
        st_ref[...] = jnp.zeros_like(st_ref)

    @pl.when((i + tiles_per_seq - 1) % tiles_per_seq == 0)
    def _():
        st_ref[...] = jnp.zeros_like(st_ref)

    def retention_units(prev_ref):
        for sub in range(ROW_TILE // RET_BLOCK):
            rows = slice(sub * RET_BLOCK, (sub + 1) * RET_BLOCK)
            for head in range(RET_HEADS):
                yield functools.partial(_retention_unit, prev_ref, rows, head, dmat_ref, qdec_ref,
                                        kdec_ref, gn_ref, st_ref, y_ref, block_decay)

    def step(cur_ref, prev_ref, project):
        units = list(retention_units(prev_ref))
        if not project:
            for unit in units:
                unit()
            return
        x = x_ref[...]
        inv = lax.rsqrt(jnp.mean(x * x, axis=-1, keepdims=True) + EPS)
        h = (x * g_ref[...]).astype(BF16)
        n_blocks = IN_WIDTH // IN_COL_TILE
        first = [(len(units) * c) // n_blocks for c in range(n_blocks + 1)]
        for c in range(n_blocks):
            acc = jnp.dot(h, w_ref[:, c * IN_COL_TILE:(c + 1) * IN_COL_TILE],
                          preferred_element_type=F32)
            if c < 2:
                cos = cos_ref[c] * inv
                sin = sin_ref[c] * inv
                for h_i in range(RET_HEADS):
                    lo = c * RET_WIDTH + h_i * RET_HEAD_DIM
                    a = acc[:, h_i * RET_HEAD_DIM:(h_i + 1) * RET_HEAD_DIM]
                    cur_ref[:, lo:lo + RET_HEAD_DIM] = (
                        a * cos + pltpu.roll(a, RET_HEAD_DIM // 2, axis=1) * sin).astype(BF16)
            elif c < 4:
                cur_ref[:, c * RET_WIDTH:(c + 1) * RET_WIDTH] = (acc * inv).astype(BF16)
            else:
                acc = acc * inv
                for k in range(SSM_SLABS):
                    u_ref[k] = acc[:, k * LANES:(k + 1) * LANES]
            for unit in units[first[c]:first[c + 1]]:
                unit()

    even = i % 2 == 0
    pl.when((i < n_tiles) & even)(lambda: step(qkvg_a, qkvg_b, True))
    pl.when((i < n_tiles) & jnp.logical_not(even))(lambda: step(qkvg_b, qkvg_a, True))
    last_prev = qkvg_b if n_tiles % 2 == 0 else qkvg_a
    pl.when(i == n_tiles)(lambda: step(None, last_prev, False))


def _retention_log_decay():
    return np.log1p(-(2.0 ** (-5.0 - np.arange(RET_HEADS, dtype=np.float64))))


def _retention_tables():
    log_g = _retention_log_decay()
    idx = np.arange(RET_BLOCK, dtype=np.float64)
    chunk = np.arange(RET_BLOCK) // CHUNK
    diff = idx[:, None] - idx[None, :]
    same = chunk[:, None] == chunk[None, :]
    earlier = chunk[None, :] < chunk[:, None]
    dist = np.where(same, np.abs(diff), diff)
    dmat = np.where((same | earlier)[None], np.exp(log_g[:, None, None] * dist[None]), 0.0)
    qdec = np.exp(log_g[:, None] * (idx + 1.0)[None, :])
    kdec = np.exp(log_g[:, None] * (RET_BLOCK - 1.0 - idx)[None, :])
    qdec = np.broadcast_to(qdec[:, :, None], (RET_HEADS, RET_BLOCK, RET_HEAD_DIM))
    kdec = np.broadcast_to(kdec[:, :, None], (RET_HEADS, RET_BLOCK, RET_HEAD_DIM))
    return tuple(jnp.asarray(a, dtype=F32) for a in (dmat, qdec, kdec))


def _mix_in(x2d, gain, w_bf16, rope, gn_gain, seq, cast=()):
    m = x2d.shape[0]
    assert IN_COL_TILE == RET_WIDTH == SSM_WIDTH and ROW_TILE % RET_BLOCK == 0
    n_tiles = m // ROW_TILE
    tiles_per_seq = seq // ROW_TILE
    dmat, qdec, kdec = _retention_tables()
    block_decay = tuple(float(v) for v in np.exp(_retention_log_decay() * RET_BLOCK))
    cur = lambda i: jnp.minimum(i, n_tiles - 1)
    cast_specs = _side_cast_specs(cast, n_tiles, cur)

    def whole(shape):
        return pl.BlockSpec(shape, lambda i: (0,) * len(shape))

    tab = pl.BlockSpec((2, ROW_TILE, RET_HEAD_DIM), lambda i: (0, cur(i) % tiles_per_seq, 0))
    outs = pl.pallas_call(
        functools.partial(_mix_in_kernel, n_tiles=n_tiles, tiles_per_seq=tiles_per_seq,
                          block_decay=block_decay, n_cast=len(cast)),
        grid=(n_tiles + 1,),
        in_specs=[
            pl.BlockSpec((ROW_TILE, D_MODEL), lambda i: (cur(i), 0)),
            whole((1, D_MODEL)),
            pl.BlockSpec((D_MODEL, IN_WIDTH), lambda i: (0, 0), pipeline_mode=pl.Buffered(1)),
            tab, tab,
            whole((RET_HEADS, RET_BLOCK, RET_BLOCK)),
            whole((RET_HEADS, RET_BLOCK, RET_HEAD_DIM)),
            whole((RET_HEADS, RET_BLOCK, RET_HEAD_DIM)),
            whole((1, RET_WIDTH)),
        ] + cast_specs,
        out_specs=[
            pl.BlockSpec((SSM_SLABS, ROW_TILE, LANES), lambda i: (0, cur(i), 0)),
            pl.BlockSpec((ROW_TILE, RET_WIDTH), lambda i: (jnp.maximum(i - 1, 0), 0)),
        ] + cast_specs,
        out_shape=[jax.ShapeDtypeStruct((SSM_SLABS, m, LANES), F32),
                   jax.ShapeDtypeStruct((m, RET_WIDTH), BF16)]
        + [jax.ShapeDtypeStruct(w.shape, BF16) for w in cast],
        scratch_shapes=[pltpu.VMEM((ROW_TILE, 4 * RET_WIDTH), BF16),
                        pltpu.VMEM((ROW_TILE, 4 * RET_WIDTH), BF16),
                        pltpu.VMEM((RET_HEADS, RET_HEAD_DIM, RET_HEAD_DIM), F32)],
        compiler_params=_params(("arbitrary",)),
        name="mix_in",
    )(x2d, gain, w_bf16, *rope, dmat, qdec, kdec, gn_gain, *cast)
    return outs[:2], outs[2:]


def _cmul(ar, ai, br, bi):
    return ar * br - ai * bi, ar * bi + ai * br


def _zoh(a_re, a_im, dt):
    e = jnp.exp(a_re * dt)
    lr = e * jnp.cos(a_im * dt)
    li = e * jnp.sin(a_im * dt)
    inv = 1.0 / (a_re * a_re + a_im * a_im)
    xr = lr - 1.0
    return lr, li, (xr * a_re + li * a_im) * inv, (li * a_re - xr * a_im) * inv


def _powers_by_bits(expo, lr, li, n_bits):
    pr = jnp.ones(expo.shape, F32)
    pi = jnp.zeros(expo.shape, F32)
    qr, qi = lr, li
    for bit in range(n_bits):
        sel = (expo & (1 << bit)) != 0
        mr, mi = _cmul(pr, pi, qr, qi)
        pr = jnp.where(sel, mr, pr)
        pi = jnp.where(sel, mi, pi)
        qr, qi = _cmul(qr, qi, qr, qi)
    return pr, pi, qr, qi


S5_PREP_INPUTS = 8
S5_PREP_OUTPUTS = 5


def _select_t(a, sel):
    return lax.dot_general(a, sel, (((0,), (0,)), ((), ())), preferred_element_type=F32,
                           precision=lax.Precision.HIGHEST)


def _s5_prep_kernel(*refs, n_cast):
    ins = refs[:S5_PREP_INPUTS]
    outs = refs[S5_PREP_INPUTS + n_cast:S5_PREP_INPUTS + n_cast + S5_PREP_OUTPUTS]
    _side_cast(refs[S5_PREP_INPUTS:S5_PREP_INPUTS + n_cast],
               refs[S5_PREP_INPUTS + n_cast + S5_PREP_OUTPUTS:])
    for g in range(SSM_GB):
        _s5_prep_group(*[r.at[g] for r in ins + outs])


def _s5_prep_group(logdt_ref, are_l_ref, aim_l_ref, bre_ref, bim_ref,
                   cre_ref, cim_ref, d_ref, w_ref, m_ref, v_ref, da_ref, db_ref):
    n_bits = SSM_T.bit_length() - 1
    dt = jnp.exp(logdt_ref[...])

    lr, li, br, bi = _zoh(are_l_ref[...], aim_l_ref[...], dt)
    dup = (lax.broadcasted_iota(jnp.int32, (SSM_STATE, LANES), 1) % SSM_STATE
           == lax.broadcasted_iota(jnp.int32, (SSM_STATE, LANES), 0)).astype(F32)
    bbr, bbi = _cmul(br, bi, _select_t(bre_ref[...], dup), _select_t(bim_ref[...], dup))
    low = lax.broadcasted_iota(jnp.int32, (SSM_GROUP, LANES), 1) < SSM_STATE
    x1 = jnp.where(low, bbr, bbi)
    x2 = jnp.where(low, -bbi, bbr)
    srow = lax.broadcasted_iota(jnp.int32, (SSM_T, LANES), 0)
    pr, pi, lr_t, li_t = _powers_by_bits(SSM_T - 1 - srow, lr, li, n_bits)
    w = jnp.concatenate([pr[s:s + 1] * x1 + pi[s:s + 1] * x2 for s in range(SSM_T)], axis=0)
    w_ref[:, :SSM_SW] = w.astype(BF16)
    w_ref[:, SSM_SW:] = pltpu.roll(w, SSM_STATE, axis=1).astype(BF16)
    da_ref[...] = lr_t
    db_ref[...] = jnp.where(low[:1], -li_t, li_t)

    diag = (lax.broadcasted_iota(jnp.int32, (SSM_STATE, LANES), 1)
            == lax.broadcasted_iota(jnp.int32, (SSM_STATE, LANES), 0))
    to_col = lambda row: jnp.sum(jnp.where(diag, row, 0.0), axis=1, keepdims=True)
    lrs, lis = to_col(lr), to_col(li)
    tau = lax.broadcasted_iota(jnp.int32, (SSM_STATE, SSM_CW), 1) // SSM_GROUP
    p0r, p0i, _, _ = _powers_by_bits(tau, lrs, lis, n_bits)
    tile = (lax.broadcasted_iota(jnp.int32, (SSM_GROUP, SSM_CW), 1) % SSM_GROUP
            == lax.broadcasted_iota(jnp.int32, (SSM_GROUP, SSM_CW), 0)).astype(F32)
    cr, ci = _select_t(cre_ref[...], tile), _select_t(cim_ref[...], tile)
    gr, gi = _cmul(cr, ci, p0r, p0i)
    lhs = jnp.where(low, bbr, -bbi)
    k_all = jnp.dot(lhs, jnp.concatenate([gr, gi], axis=0),
                    preferred_element_type=F32, precision=lax.Precision.HIGHEST)
    crow = lax.broadcasted_iota(jnp.int32, (SSM_GROUP, SSM_CW), 0)
    clane = lax.broadcasted_iota(jnp.int32, (SSM_GROUP, SSM_CW), 1)
    k_all = k_all + jnp.where(crow == clane, d_ref[...], 0.0)
    for s in range(SSM_T):
        shifted = k_all if s == 0 else pltpu.roll(k_all, s * SSM_GROUP, axis=1)
        m_ref[s * SSM_GROUP:(s + 1) * SSM_GROUP, :] = jnp.where(
            clane >= s * SSM_GROUP, shifted, 0.0).astype(BF16)
    p1r, p1i = _cmul(p0r, p0i, lrs, lis)
    vr, vi = _cmul(cr, ci, p1r, p1i)
    v_ref[:SSM_STATE, :] = vr.astype(BF16)
    v_ref[SSM_STATE:, :] = (-vi).astype(BF16)


def _s5_prep(a_re, a_im, log_dt, b_re, b_im, c_re, c_im, d_skip, cast=()):
    g = SSM_GROUPS
    cast_specs = _side_cast_specs(cast, g // SSM_GB, lambda i: i)
    dup = lambda a: jnp.concatenate([a, a], axis=-1)
    args = (
        log_dt.reshape(g, 1, 1),
        dup(a_re)[:, None, :], dup(a_im)[:, None, :],
        b_re, b_im,
        c_re, c_im,
        d_skip.reshape(g, SSM_GROUP, 1),
    )

    def spec(a):
        return pl.BlockSpec((SSM_GB,) + a.shape[1:], lambda i: (i, 0, 0))

    out_shapes = [
        jax.ShapeDtypeStruct((g, SSM_CW, 2 * SSM_SW), BF16),
        jax.ShapeDtypeStruct((g, SSM_CW, SSM_CW), BF16),
        jax.ShapeDtypeStruct((g, SSM_SW, SSM_CW), BF16),
        jax.ShapeDtypeStruct((g, 1, SSM_SW), F32),
        jax.ShapeDtypeStruct((g, 1, SSM_SW), F32),
    ]
    assert len(args) == S5_PREP_INPUTS and len(out_shapes) == S5_PREP_OUTPUTS
    outs = pl.pallas_call(
        functools.partial(_s5_prep_kernel, n_cast=len(cast)),
        grid=(g // SSM_GB,),
        in_specs=[spec(a) for a in args] + cast_specs,
        out_specs=[spec(s) for s in out_shapes] + cast_specs,
        out_shape=out_shapes + [jax.ShapeDtypeStruct(w.shape, BF16) for w in cast],
        compiler_params=_params(("parallel",)),
        name="s5_prep",
    )(*args, *cast)
    wmat, mmat, vmat, dec_a, dec_b = outs[:S5_PREP_OUTPUTS]
    mats = (wmat, mmat, vmat, dec_a.reshape(g, SSM_SW), dec_b.reshape(g, SSM_SW))
    return mats, outs[S5_PREP_OUTPUTS:]


def _chunk_permutation():
    n = SSM_GB * LANES
    t, g, c = np.meshgrid(np.arange(SSM_GB), np.arange(SSM_GB), np.arange(SSM_GROUP),
                          indexing="ij")
    p = np.zeros((n, n), np.float32)
    p[(t * LANES + g * SSM_GROUP + c).ravel(), (g * LANES + t * SSM_GROUP + c).ravel()] = 1.0
    return jnp.asarray(p, dtype=BF16)


def _s5_kernel(u_ref, p_ref, w_ref, m_ref, v_ref, da_ref, db_ref, *rest, n_chunks, n_cast):
    y_ref = rest[n_cast]
    cat_ref, uc_ref, wx_ref, ws_ref, xp_ref = rest[2 * n_cast + 1:]
    _side_cast(rest[:n_cast], rest[n_cast + 1:2 * n_cast + 1])
    halves = SSM_T // SSM_GB
    half_w = SSM_GB * LANES

    for t in range(SSM_T):
        cat_ref[:, t * LANES:(t + 1) * LANES] = (
            u_ref[pl.ds(t, n_chunks, stride=SSM_T), :].astype(BF16))
    for half in range(halves):
        z = jnp.dot(cat_ref[:, half * half_w:(half + 1) * half_w], p_ref[...],
                    preferred_element_type=F32).astype(BF16)
        for g in range(SSM_GB):
            uc_ref[g, :, half * LANES:(half + 1) * LANES] = z[:, g * LANES:(g + 1) * LANES]

    for g in range(SSM_GB):
        w = jnp.dot(uc_ref[g], w_ref[g], preferred_element_type=F32)
        wx_ref[pl.ds(g, n_chunks, stride=SSM_GB), :] = w[:, :SSM_SW]
        ws_ref[pl.ds(g, n_chunks, stride=SSM_GB), :] = w[:, SSM_SW:]

    dec_a = da_ref[...]
    dec_b = db_ref[...]

    dec2_a = dec_a * dec_a - dec_b * dec_b
    dec2_b = 2.0 * dec_a * dec_b

    def step(n, carry):
        x, xs = carry
        r0 = pl.ds(pl.multiple_of(2 * n * SSM_GB, SSM_GB), SSM_GB)
        r1 = pl.ds(pl.multiple_of((2 * n + 1) * SSM_GB, SSM_GB), SSM_GB)
        w0, w0s = wx_ref[r0, :], ws_ref[r0, :]
        c = dec_a * w0 + dec_b * w0s + wx_ref[r1, :]
        cs = dec_a * w0s - dec_b * w0 + ws_ref[r1, :]
        xp_ref[r0, :] = x
        xp_ref[r1, :] = dec_a * x + dec_b * xs + w0
        return dec2_a * x + dec2_b * xs + c, dec2_a * xs - dec2_b * x + cs

    zero = jnp.zeros((SSM_GB, SSM_SW), F32)
    lax.fori_loop(0, n_chunks // 2, step, (zero, zero), unroll=4)

    for g in range(SSM_GB):
        y = jnp.dot(uc_ref[g], m_ref[g], preferred_element_type=F32)
        xp = xp_ref[pl.ds(g, n_chunks, stride=SSM_GB), :].astype(BF16)
        y = (y + jnp.dot(xp, v_ref[g], preferred_element_type=F32)).astype(BF16)
        for half in range(halves):
            cat_ref[:, half * half_w + g * LANES:half * half_w + (g + 1) * LANES] = (
                y[:, half * LANES:(half + 1) * LANES])

    for half in range(halves):
        z = jnp.dot(cat_ref[:, half * half_w:(half + 1) * half_w], p_ref[...],
                    preferred_element_type=F32)
        for t in range(SSM_GB):
            y_ref[pl.ds(half * SSM_GB + t, n_chunks, stride=SSM_T), :] = (
                z[:, t * LANES:(t + 1) * LANES])


def _s5(u_slabs, mats, batch, seq, cast=()):
    wmat, mmat, vmat, dec_a, dec_b = mats
    m = u_slabs.shape[1]
    n_chunks = seq // SSM_T
    perm = _chunk_permutation()
    cast_specs = _side_cast_specs(cast, SSM_SLABS * batch, lambda k, b: k * batch + b)

    def grp(shape):
        return pl.BlockSpec((SSM_GB,) + shape, lambda k, b: (k,) + (0,) * len(shape))

    seq_spec = pl.BlockSpec((None, seq, LANES), lambda k, b: (k, b, 0))
    outs = pl.pallas_call(
        functools.partial(_s5_kernel, n_chunks=n_chunks, n_cast=len(cast)),
        grid=(SSM_SLABS, batch),
        in_specs=[
            seq_spec,
            pl.BlockSpec(perm.shape, lambda k, b: (0, 0)),
            grp((SSM_CW, 2 * SSM_SW)),
            grp((SSM_CW, SSM_CW)),
            grp((SSM_SW, SSM_CW)),
            grp((SSM_SW,)),
            grp((SSM_SW,)),
        ] + cast_specs,
        out_specs=[seq_spec] + cast_specs,
        out_shape=[jax.ShapeDtypeStruct((SSM_SLABS, m, LANES), F32)]
        + [jax.ShapeDtypeStruct(w.shape, BF16) for w in cast],
        scratch_shapes=[
            pltpu.VMEM((n_chunks, SSM_T * LANES), BF16),
            pltpu.VMEM((SSM_GB, n_chunks, SSM_CW), BF16),
            pltpu.VMEM((SSM_GB * n_chunks, SSM_SW), F32),
            pltpu.VMEM((SSM_GB * n_chunks, SSM_SW), F32),
            pltpu.VMEM((SSM_GB * n_chunks, SSM_SW), F32),
        ],
        compiler_params=_params(("parallel", "parallel")),
        name="s5_chunks",
    )(u_slabs, perm, wmat, mmat, vmat, dec_a, dec_b, *cast)
    return outs[0], outs[1:]


def _outproj_kernel(x_ref, yr_ref, ys_ref, wglu_ref, bglu_ref, og_ref, wout_ref, *rest, n_cast):
    o_ref = rest[n_cast]
    _side_cast(rest[:n_cast], rest[n_cast + 1:])
    half = D_MODEL // 2
    yr = yr_ref[...]
    acc_lo = jnp.dot(yr, wout_ref[:RET_WIDTH, :half], preferred_element_type=F32)
    y1 = jax.nn.gelu(jnp.concatenate([ys_ref[k] for k in range(SSM_SLABS)], axis=-1))
    z = jnp.dot(y1.astype(BF16), wglu_ref[...], preferred_element_type=F32) + bglu_ref[...]
    acc_hi = jnp.dot(yr, wout_ref[:RET_WIDTH, half:], preferred_element_type=F32)
    y2 = y1 * jax.nn.sigmoid(z)
    y_ssm = _rms(y2, og_ref[...]).astype(BF16)
    o_ref[:, :half] = x_ref[:, :half] + acc_lo + jnp.dot(
        y_ssm, wout_ref[RET_WIDTH:, :half], preferred_element_type=F32)
    o_ref[:, half:] = x_ref[:, half:] + acc_hi + jnp.dot(
        y_ssm, wout_ref[RET_WIDTH:, half:], preferred_element_type=F32)


def _outproj(x2d, y_ret, y_s5, w_glu, b_glu, out_g, w_out, cast=()):
    m = x2d.shape[0]
    cast_specs = _side_cast_specs(cast, m // ROW_TILE, lambda i: i)

    def rows(width):
        return pl.BlockSpec((ROW_TILE, width), lambda i: (i, 0))

    def whole(shape):
        return pl.BlockSpec(shape, lambda i: (0, 0))

    outs = pl.pallas_call(
        functools.partial(_outproj_kernel, n_cast=len(cast)),
        grid=(m // ROW_TILE,),
        in_specs=[
            rows(D_MODEL), rows(RET_WIDTH),
            pl.BlockSpec((SSM_SLABS, ROW_TILE, LANES), lambda i: (0, i, 0)),
            whole((SSM_WIDTH, SSM_WIDTH)), whole((1, SSM_WIDTH)), whole((1, SSM_WIDTH)),
            whole((D_MODEL, D_MODEL)),
        ] + cast_specs,
        out_specs=[rows(D_MODEL)] + cast_specs,
        out_shape=[jax.ShapeDtypeStruct((m, D_MODEL), F32)]
        + [jax.ShapeDtypeStruct(w.shape, BF16) for w in cast],
        compiler_params=_params(("parallel",)),
        name="outproj",
    )(x2d, y_ret, y_s5, w_glu, b_glu, out_g, w_out, *cast)
    return outs[0], outs[1:]


def _ffn_kernel(x_ref, g_ref, wg_ref, wu_ref, wd_ref, gf_ref, o_ref, h_ref):
    j = pl.program_id(1)
    last = pl.num_programs(1) - 1

    def step(first, final):
        if first:
            x = x_ref[...]
            h = _rms(x, g_ref[...]).astype(BF16)
            h_ref[...] = h
            base = x
        else:
            h = h_ref[...]
            base = o_ref[...]
        o = base
        acts = []
        for s in range(FF_SPLIT):
            cols = slice(s * (FF_TILE // FF_SPLIT), (s + 1) * (FF_TILE // FF_SPLIT))
            gate = jnp.dot(h, wg_ref[:, cols], preferred_element_type=F32)
            up = jnp.dot(h, wu_ref[:, cols], preferred_element_type=F32)
            acts.append((cols, (jax.nn.silu(gate) * up).astype(BF16)))
        for cols, act in acts:
            o = o + jnp.dot(act, wd_ref[cols, :], preferred_element_type=F32)
        o_ref[...] = _rms(o, gf_ref[...]) if final else o

    pl.when(j == 0)(lambda: step(True, False))
    pl.when((j > 0) & (j < last))(lambda: step(False, False))
    pl.when(j == last)(lambda: step(False, True))


def _ffn(x2d, gain, w_gate, w_up, w_down, gain_final):
    m = x2d.shape[0]
    return pl.pallas_call(
        _ffn_kernel,
        grid=(m // FFN_ROW_TILE, D_FF // FF_TILE),
        in_specs=[
            pl.BlockSpec((FFN_ROW_TILE, D_MODEL), lambda i, j: (i, 0)),
            pl.BlockSpec((1, D_MODEL), lambda i, j: (0, 0)),
            pl.BlockSpec((D_MODEL, FF_TILE), lambda i, j: (0, j)),
            pl.BlockSpec((D_MODEL, FF_TILE), lambda i, j: (0, j)),
            pl.BlockSpec((FF_TILE, D_MODEL), lambda i, j: (j, 0)),
            pl.BlockSpec((1, D_MODEL), lambda i, j: (0, 0)),
        ],
        out_specs=pl.BlockSpec((FFN_ROW_TILE, D_MODEL), lambda i, j: (i, 0)),
        out_shape=jax.ShapeDtypeStruct((m, D_MODEL), F32),
        scratch_shapes=[pltpu.VMEM((FFN_ROW_TILE, D_MODEL), BF16)],
        compiler_params=_params(("parallel", "arbitrary")),
        name="ffn",
    )(x2d, gain, w_gate, w_up, w_down, gain_final)


def _rope_tables(seq):
    half = RET_HEAD_DIM // 2
    pos = np.arange(seq, dtype=np.float64)
    freqs = ROPE_BASE ** (-np.arange(half, dtype=np.float64) / half)
    ang = pos[:, None] * freqs[None, :]
    cos = np.concatenate([np.cos(ang), np.cos(ang)], axis=-1)
    sin = np.concatenate([-np.sin(ang), np.sin(ang)], axis=-1)
    scale = np.array([1.0, RET_HEAD_DIM ** -0.5])[:, None, None]
    return (jnp.asarray(cos[None] * scale, dtype=F32), jnp.asarray(sin[None] * scale, dtype=F32))


def kernel(x, norm_mix_g, w_in, ret_gn_g, ssm_a_re, ssm_a_im, ssm_log_dt, ssm_b_re, ssm_b_im,
           ssm_c_re, ssm_c_im, ssm_d, ssm_w_glu, ssm_b_glu, ssm_out_g, w_out, norm_ffn_g,
           w_gate, w_up, w_down, norm_final_g):
    batch, seq, d = x.shape
    depth = w_in.shape[0]
    assert d == D_MODEL and seq % ROW_TILE == 0 and seq % RET_BLOCK == 0 and seq % SSM_T == 0
    m = batch * seq
    rope = _rope_tables(seq)
    x2d = x.reshape(m, d)

    for l in range(depth):
        mats, (w_in_b, w_out_b, w_glu_b) = _s5_prep(
            ssm_a_re[l], ssm_a_im[l], ssm_log_dt[l], ssm_b_re[l], ssm_b_im[l], ssm_c_re[l],
            ssm_c_im[l], ssm_d[l], cast=(w_in[l], w_out[l], ssm_w_glu[l]))
        (u_slabs, y_ret), (w_gate_b,) = _mix_in(x2d, norm_mix_g[l][None], w_in_b, rope,
                                                ret_gn_g[l][None], seq, cast=(w_gate[l],))
        y_s5, (w_down_b,) = _s5(u_slabs, mats, batch, seq, cast=(w_down[l],))

        x2d, (w_up_b,) = _outproj(x2d, y_ret, y_s5, w_glu_b, ssm_b_glu[l][None],
                                  ssm_out_g[l][None], w_out_b, cast=(w_up[l],))
        last = l == depth - 1
        assert last, "fused final norm assumes a single layer"
        x2d = _ffn(x2d, norm_ffn_g[l][None], w_gate_b, w_up_b, w_down_b, norm_final_g[None])
    return x2d.reshape(batch, seq, d)
```

```python
import functools

import jax
import jax.numpy as jnp
import numpy as np
from jax import lax
from jax.experimental import pallas as pl
from jax.experimental.pallas import tpu as pltpu

D_MODEL = 2048
CHUNK = 64
RET_WIDTH = D_MODEL // 2
RET_HEADS = 8
RET_HEAD_DIM = RET_WIDTH // RET_HEADS
SSM_WIDTH = D_MODEL - RET_WIDTH
SSM_GROUP = 16
SSM_GROUPS = SSM_WIDTH // SSM_GROUP
SSM_STATE = 64
D_FF = -(-8 * D_MODEL // (3 * 256)) * 256
IN_WIDTH = 4 * RET_WIDTH + SSM_WIDTH
ROPE_BASE = 10000.0
EPS = 1e-6

F32 = jnp.float32
BF16 = jnp.bfloat16

V7X_VMEM_BYTES = 64 * 1024 * 1024
VMEM_LIMIT = V7X_VMEM_BYTES - 4 * 1024 * 1024

ROW_TILE = 512
IN_COL_TILE = RET_WIDTH
FFN_ROW_TILE = 1024
FF_TILE = 512
FF_SPLIT = 2
RET_BLOCK = 256
SSM_T = 16
SSM_CW = SSM_T * SSM_GROUP
LANES = 128
BF16_ROWS = 16
SSM_GB = LANES // SSM_GROUP
SSM_SLABS = SSM_GROUPS // SSM_GB
SSM_SW = 2 * SSM_STATE


def _params(sem):
    return pltpu.CompilerParams(dimension_semantics=sem, vmem_limit_bytes=VMEM_LIMIT)


def _rms(x, g):
    return x * lax.rsqrt(jnp.mean(x * x, axis=-1, keepdims=True) + EPS) * g


def _side_cast_specs(weights, n_steps, step_of):
    specs = []
    for w in weights:
        rows, cols = w.shape
        assert rows % (n_steps * BF16_ROWS) == 0, (w.shape, n_steps)
        specs.append(pl.BlockSpec((rows // n_steps, cols),
                                  lambda *idx, step_of=step_of: (step_of(*idx), 0)))
    return specs


def _side_cast(srcs, dsts):
    for src, dst in zip(srcs, dsts):
        dst[...] = src[...].astype(BF16)


def _retention_unit(qkvg_ref, rows, head, dmat_ref, qdec_ref, kdec_ref, gn_ref, st_ref, y_ref,
                    block_decay):
    col = head * RET_HEAD_DIM
    hs = slice(col, col + RET_HEAD_DIM)
    q = qkvg_ref[rows, col:col + RET_HEAD_DIM]
    k = qkvg_ref[rows, RET_WIDTH + col:RET_WIDTH + col + RET_HEAD_DIM]
    v = qkvg_ref[rows, 2 * RET_WIDTH + col:2 * RET_WIDTH + col + RET_HEAD_DIM]
    s = lax.dot_general(q, k, (((1,), (1,)), ((), ())),
                        preferred_element_type=F32) * dmat_ref[head]
    out = jnp.dot(s.astype(BF16), v, preferred_element_type=F32)
    state = st_ref[head]
    out = out + jnp.dot(q, state.astype(BF16), preferred_element_type=F32) * qdec_ref[head]
    kv = lax.dot_general(k * kdec_ref[head].astype(BF16), v, (((0,), (0,)), ((), ())),
                         preferred_element_type=F32)
    st_ref[head] = state * block_decay[head] + kv
    mu = jnp.mean(out, axis=-1, keepdims=True)
    cen = out - mu
    var = jnp.mean(cen * cen, axis=-1, keepdims=True)
    normed = cen * lax.rsqrt(var + EPS) * gn_ref[:, hs]
    gate = qkvg_ref[rows, 3 * RET_WIDTH + col:3 * RET_WIDTH + col + RET_HEAD_DIM].astype(F32)
    y_ref[rows, hs] = (jax.nn.silu(gate) * normed).astype(BF16)


def _mix_in_kernel(x_ref, g_ref, w_ref, cos_ref, sin_ref, dmat_ref, qdec_ref, kdec_ref, gn_ref,
                   *rest, n_tiles, tiles_per_seq, block_decay, n_cast):
    u_ref, y_ref = rest[n_cast:n_cast + 2]
    qkvg_a, qkvg_b, st_ref = rest[-3:]
    _side_cast(rest[:n_cast], rest[n_cast + 2:-3])
    i = pl.program_id(0)

    @pl.when(i == 0)
    def _():
        qkvg_b[...] = jnp.zeros_like(qkvg_b)
        st_ref[...] = jnp.zeros_like(st_ref)

    @pl.when((i + tiles_per_seq - 1) % tiles_per_seq == 0)
    def _():
        st_ref[...] = jnp.zeros_like(st_ref)

    def retention_units(prev_ref):
        for sub in range(ROW_TILE // RET_BLOCK):
            rows = slice(sub * RET_BLOCK, (sub + 1) * RET_BLOCK)
            for head in range(RET_HEADS):
                yield functools.partial(_retention_unit, prev_ref, rows, head, dmat_ref, qdec_ref,
                                        kdec_ref, gn_ref, st_ref, y_ref, block_decay)

    def step(cur_ref, prev_ref, project):
        units = list(retention_units(prev_ref))
        if not project:
            for unit in units:
                unit()
            return
        x = x_ref[...]
        inv = lax.rsqrt(jnp.mean(x * x, axis=-1, keepdims=True) + EPS)
        h = (x * g_ref[...]).astype(BF16)
        n_blocks = IN_WIDTH // IN_COL_TILE
        first = [(len(units) * c) // n_blocks for c in range(n_blocks + 1)]
        for c in range(n_blocks):
            acc = jnp.dot(h, w_ref[:, c * IN_COL_TILE:(c + 1) * IN_COL_TILE],
                          preferred_element_type=F32)
            if c < 2:
                cos = cos_ref[c] * inv
                sin = sin_ref[c] * inv
                for h_i in range(RET_HEADS):
                    lo = c * RET_WIDTH + h_i * RET_HEAD_DIM
                    a = acc[:, h_i * RET_HEAD_DIM:(h_i + 1) * RET_HEAD_DIM]
                    cur_ref[:, lo:lo + RET_HEAD_DIM] = (
                        a * cos + pltpu.roll(a, RET_HEAD_DIM // 2, axis=1) * sin).astype(BF16)
            elif c < 4:
                cur_ref[:, c * RET_WIDTH:(c + 1) * RET_WIDTH] = (acc * inv).astype(BF16)
            else:
                acc = acc * inv
                for k in range(SSM_SLABS):
                    u_ref[k] = acc[:, k * LANES:(k + 1) * LANES]
            for unit in units[first[c]:first[c + 1]]:
                unit()

    even = i % 2 == 0
    pl.when((i < n_tiles) & even)(lambda: step(qkvg_a, qkvg_b, True))
    pl.when((i < n_tiles) & jnp.logical_not(even))(lambda: step(qkvg_b, qkvg_a, True))
    last_prev = qkvg_b if n_tiles % 2 == 0 else qkvg_a
    pl.when(i == n_tiles)(lambda: step(None, last_prev, False))


def _retention_log_decay():
    return np.log1p(-(2.0 ** (-5.0 - np.arange(RET_HEADS, dtype=np.float64))))


def _retention_tables():
    log_g = _retention_log_decay()
    idx = np.arange(RET_BLOCK, dtype=np.float64)
    chunk = np.arange(RET_BLOCK) // CHUNK
    diff = idx[:, None] - idx[None, :]
    same = chunk[:, None] == chunk[None, :]
    earlier = chunk[None, :] < chunk[:, None]
    dist = np.where(same, np.abs(diff), diff)
    dmat = np.where((same | earlier)[None], np.exp(log_g[:, None, None] * dist[None]), 0.0)
    qdec = np.exp(log_g[:, None] * (idx + 1.0)[None, :])
    kdec = np.exp(log_g[:, None] * (RET_BLOCK - 1.0 - idx)[None, :])
    qdec = np.broadcast_to(qdec[:, :, None], (RET_HEADS, RET_BLOCK, RET_HEAD_DIM))
    kdec = np.broadcast_to(kdec[:, :, None], (RET_HEADS, RET_BLOCK, RET_HEAD_DIM))
    return tuple(jnp.asarray(a, dtype=F32) for a in (dmat, qdec, kdec))


def _mix_in(x2d, gain, w_bf16, rope, gn_gain, seq, cast=()):
    m = x2d.shape[0]
    assert IN_COL_TILE == RET_WIDTH == SSM_WIDTH and ROW_TILE % RET_BLOCK == 0
    n_tiles = m // ROW_TILE
    tiles_per_seq = seq // ROW_TILE
    dmat, qdec, kdec = _retention_tables()
    block_decay = tuple(float(v) for v in np.exp(_retention_log_decay() * RET_BLOCK))
    cur = lambda i: jnp.minimum(i, n_tiles - 1)
    cast_specs = _side_cast_specs(cast, n_tiles, cur)

    def whole(shape):
        return pl.BlockSpec(shape, lambda i: (0,) * len(shape))

    tab = pl.BlockSpec((2, ROW_TILE, RET_HEAD_DIM), lambda i: (0, cur(i) % tiles_per_seq, 0))
    outs = pl.pallas_call(
        functools.partial(_mix_in_kernel, n_tiles=n_tiles, tiles_per_seq=tiles_per_seq,
                          block_decay=block_decay, n_cast=len(cast)),
        grid=(n_tiles + 1,),
        in_specs=[
            pl.BlockSpec((ROW_TILE, D_MODEL), lambda i: (cur(i), 0)),
            whole((1, D_MODEL)),
            pl.BlockSpec((D_MODEL, IN_WIDTH), lambda i: (0, 0), pipeline_mode=pl.Buffered(1)),
            tab, tab,
            whole((RET_HEADS, RET_BLOCK, RET_BLOCK)),
            whole((RET_HEADS, RET_BLOCK, RET_HEAD_DIM)),
            whole((RET_HEADS, RET_BLOCK, RET_HEAD_DIM)),
            whole((1, RET_WIDTH)),
        ] + cast_specs,
        out_specs=[
            pl.BlockSpec((SSM_SLABS, ROW_TILE, LANES), lambda i: (0, cur(i), 0)),
            pl.BlockSpec((ROW_TILE, RET_WIDTH), lambda i: (jnp.maximum(i - 1, 0), 0)),
        ] + cast_specs,
        out_shape=[jax.ShapeDtypeStruct((SSM_SLABS, m, LANES), F32),
                   jax.ShapeDtypeStruct((m, RET_WIDTH), BF16)]
        + [jax.ShapeDtypeStruct(w.shape, BF16) for w in cast],
        scratch_shapes=[pltpu.VMEM((ROW_TILE, 4 * RET_WIDTH), BF16),
                        pltpu.VMEM((ROW_TILE, 4 * RET_WIDTH), BF16),
                        pltpu.VMEM((RET_HEADS, RET_HEAD_DIM, RET_HEAD_DIM), F32)],
        compiler_params=_params(("arbitrary",)),
        name="mix_in",
    )(x2d, gain, w_bf16, *rope, dmat, qdec, kdec, gn_gain, *cast)
    return outs[:2], outs[2:]


def _cmul(ar, ai, br, bi):
    return ar * br - ai * bi, ar * bi + ai * br


def _zoh(a_re, a_im, dt):
    e = jnp.exp(a_re * dt)
    lr = e * jnp.cos(a_im * dt)
    li = e * jnp.sin(a_im * dt)
    inv = 1.0 / (a_re * a_re + a_im * a_im)
    xr = lr - 1.0
    return lr, li, (xr * a_re + li * a_im) * inv, (li * a_re - xr * a_im) * inv


def _powers_by_bits(expo, lr, li, n_bits):
    pr = jnp.ones(expo.shape, F32)
    pi = jnp.zeros(expo.shape, F32)
    qr, qi = lr, li
    for bit in range(n_bits):
        sel = (expo & (1 << bit)) != 0
        mr, mi = _cmul(pr, pi, qr, qi)
        pr = jnp.where(sel, mr, pr)
        pi = jnp.where(sel, mi, pi)
        qr, qi = _cmul(qr, qi, qr, qi)
    return pr, pi, qr, qi


S5_PREP_INPUTS = 8
S5_PREP_OUTPUTS = 5


def _select_t(a, sel):
    return lax.dot_general(a, sel, (((0,), (0,)), ((), ())), preferred_element_type=F32,
                           precision=lax.Precision.HIGHEST)


def _s5_prep_kernel(*refs, n_cast):
    ins = refs[:S5_PREP_INPUTS]
    outs = refs[S5_PREP_INPUTS + n_cast:S5_PREP_INPUTS + n_cast + S5_PREP_OUTPUTS]
    _side_cast(refs[S5_PREP_INPUTS:S5_PREP_INPUTS + n_cast],
               refs[S5_PREP_INPUTS + n_cast + S5_PREP_OUTPUTS:])
    for g in range(SSM_GB):
        _s5_prep_group(*[r.at[g] for r in ins + outs])


def _s5_prep_group(logdt_ref, are_l_ref, aim_l_ref, bre_ref, bim_ref,
                   cre_ref, cim_ref, d_ref, w_ref, m_ref, v_ref, da_ref, db_ref):
    n_bits = SSM_T.bit_length() - 1
    dt = jnp.exp(logdt_ref[...])

    lr, li, br, bi = _zoh(are_l_ref[...], aim_l_ref[...], dt)
    dup = (lax.broadcasted_iota(jnp.int32, (SSM_STATE, LANES), 1) % SSM_STATE
           == lax.broadcasted_iota(jnp.int32, (SSM_STATE, LANES), 0)).astype(F32)
    bbr, bbi = _cmul(br, bi, _select_t(bre_ref[...], dup), _select_t(bim_ref[...], dup))
    low = lax.broadcasted_iota(jnp.int32, (SSM_GROUP, LANES), 1) < SSM_STATE
    x1 = jnp.where(low, bbr, bbi)
    x2 = jnp.where(low, -bbi, bbr)
    srow = lax.broadcasted_iota(jnp.int32, (SSM_T, LANES), 0)
    pr, pi, lr_t, li_t = _powers_by_bits(SSM_T - 1 - srow, lr, li, n_bits)
    w = jnp.concatenate([pr[s:s + 1] * x1 + pi[s:s + 1] * x2 for s in range(SSM_T)], axis=0)
    w_ref[:, :SSM_SW] = w.astype(BF16)
    w_ref[:, SSM_SW:] = pltpu.roll(w, SSM_STATE, axis=1).astype(BF16)
    da_ref[...] = lr_t
    db_ref[...] = jnp.where(low[:1], -li_t, li_t)

    diag = (lax.broadcasted_iota(jnp.int32, (SSM_STATE, LANES), 1)
            == lax.broadcasted_iota(jnp.int32, (SSM_STATE, LANES), 0))
    to_col = lambda row: jnp.sum(jnp.where(diag, row, 0.0), axis=1, keepdims=True)
    lrs, lis = to_col(lr), to_col(li)
    tau = lax.broadcasted_iota(jnp.int32, (SSM_STATE, SSM_CW), 1) // SSM_GROUP
    p0r, p0i, _, _ = _powers_by_bits(tau, lrs, lis, n_bits)
    tile = (lax.broadcasted_iota(jnp.int32, (SSM_GROUP, SSM_CW), 1) % SSM_GROUP
            == lax.broadcasted_iota(jnp.int32, (SSM_GROUP, SSM_CW), 0)).astype(F32)
    cr, ci = _select_t(cre_ref[...], tile), _select_t(cim_ref[...], tile)
    gr, gi = _cmul(cr, ci, p0r, p0i)
    lhs = jnp.where(low, bbr, -bbi)
    k_all = jnp.dot(lhs, jnp.concatenate([gr, gi], axis=0),
                    preferred_element_type=F32, precision=lax.Precision.HIGHEST)
    crow = lax.broadcasted_iota(jnp.int32, (SSM_GROUP, SSM_CW), 0)
    clane = lax.broadcasted_iota(jnp.int32, (SSM_GROUP, SSM_CW), 1)
    k_all = k_all + jnp.where(crow == clane, d_ref[...], 0.0)
    for s in range(SSM_T):
        shifted = k_all if s == 0 else pltpu.roll(k_all, s * SSM_GROUP, axis=1)
        m_ref[s * SSM_GROUP:(s + 1) * SSM_GROUP, :] = jnp.where(
            clane >= s * SSM_GROUP, shifted, 0.0).astype(BF16)
    p1r, p1i = _cmul(p0r, p0i, lrs, lis)
    vr, vi = _cmul(cr, ci, p1r, p1i)
    v_ref[:SSM_STATE, :] = vr.astype(BF16)
    v_ref[SSM_STATE:, :] = (-vi).astype(BF16)


def _s5_prep(a_re, a_im, log_dt, b_re, b_im, c_re, c_im, d_skip, cast=()):
    g = SSM_GROUPS
    cast_specs = _side_cast_specs(cast, g // SSM_GB, lambda i: i)
    dup = lambda a: jnp.concatenate([a, a], axis=-1)
    args = (
        log_dt.reshape(g, 1, 1),
        dup(a_re)[:, None, :], dup(a_im)[:, None, :],
        b_re, b_im,
        c_re, c_im,
        d_skip.reshape(g, SSM_GROUP, 1),
    )

    def spec(a):
        return pl.BlockSpec((SSM_GB,) + a.shape[1:], lambda i: (i, 0, 0))

    out_shapes = [
        jax.ShapeDtypeStruct((g, SSM_CW, 2 * SSM_SW), BF16),
        jax.ShapeDtypeStruct((g, SSM_CW, SSM_CW), BF16),
        jax.ShapeDtypeStruct((g, SSM_SW, SSM_CW), BF16),
        jax.ShapeDtypeStruct((g, 1, SSM_SW), F32),
        jax.ShapeDtypeStruct((g, 1, SSM_SW), F32),
    ]
    assert len(args) == S5_PREP_INPUTS and len(out_shapes) == S5_PREP_OUTPUTS
    outs = pl.pallas_call(
        functools.partial(_s5_prep_kernel, n_cast=len(cast)),
        grid=(g // SSM_GB,),
        in_specs=[spec(a) for a in args] + cast_specs,
        out_specs=[spec(s) for s in out_shapes] + cast_specs,
        out_shape=out_shapes + [jax.ShapeDtypeStruct(w.shape, BF16) for w in cast],
        compiler_params=_params(("parallel",)),
        name="s5_prep",
    )(*args, *cast)
    wmat, mmat, vmat, dec_a, dec_b = outs[:S5_PREP_OUTPUTS]
    mats = (wmat, mmat, vmat, dec_a.reshape(g, SSM_SW), dec_b.reshape(g, SSM_SW))
    return mats, outs[S5_PREP_OUTPUTS:]


def _chunk_permutation():
    n = SSM_GB * LANES
    t, g, c = np.meshgrid(np.arange(SSM_GB), np.arange(SSM_GB), np.arange(SSM_GROUP),
                          indexing="ij")
    p = np.zeros((n, n), np.float32)
    p[(t * LANES + g * SSM_GROUP + c).ravel(), (g * LANES + t * SSM_GROUP + c).ravel()] = 1.0
    return jnp.asarray(p, dtype=BF16)


def _s5_kernel(u_ref, p_ref, w_ref, m_ref, v_ref, da_ref, db_ref, *rest, n_chunks, n_cast):
    y_ref = rest[n_cast]
    cat_ref, uc_ref, wx_ref, ws_ref, xp_ref = rest[2 * n_cast + 1:]
    _side_cast(rest[:n_cast], rest[n_cast + 1:2 * n_cast + 1])
    halves = SSM_T // SSM_GB
    half_w = SSM_GB * LANES

    for t in range(SSM_T):
        cat_ref[:, t * LANES:(t + 1) * LANES] = (
            u_ref[pl.ds(t, n_chunks, stride=SSM_T), :].astype(BF16))
    for half in range(halves):
        z = jnp.dot(cat_ref[:, half * half_w:(half + 1) * half_w], p_ref[...],
                    preferred_element_type=F32).astype(BF16)
        for g in range(SSM_GB):
            uc_ref[g, :, half * LANES:(half + 1) * LANES] = z[:, g * LANES:(g + 1) * LANES]

    for g in range(SSM_GB):
        w = jnp.dot(uc_ref[g], w_ref[g], preferred_element_type=F32)
        wx_ref[pl.ds(g, n_chunks, stride=SSM_GB), :] = w[:, :SSM_SW]
        ws_ref[pl.ds(g, n_chunks, stride=SSM_GB), :] = w[:, SSM_SW:]

    dec_a = da_ref[...]
    dec_b = db_ref[...]

    dec2_a = dec_a * dec_a - dec_b * dec_b
    dec2_b = 2.0 * dec_a * dec_b

    def step(n, carry):
        x, xs = carry
        r0 = pl.ds(pl.multiple_of(2 * n * SSM_GB, SSM_GB), SSM_GB)
        r1 = pl.ds(pl.multiple_of((2 * n + 1) * SSM_GB, SSM_GB), SSM_GB)
        w0, w0s = wx_ref[r0, :], ws_ref[r0, :]
        c = dec_a * w0 + dec_b * w0s + wx_ref[r1, :]
        cs = dec_a * w0s - dec_b * w0 + ws_ref[r1, :]
        xp_ref[r0, :] = x
        xp_ref[r1, :] = dec_a * x + dec_b * xs + w0
        return dec2_a * x + dec2_b * xs + c, dec2_a * xs - dec2_b * x + cs

    zero = jnp.zeros((SSM_GB, SSM_SW), F32)
    lax.fori_loop(0, n_chunks // 2, step, (zero, zero), unroll=4)

    for g in range(SSM_GB):
        y = jnp.dot(uc_ref[g], m_ref[g], preferred_element_type=F32)
        xp = xp_ref[pl.ds(g, n_chunks, stride=SSM_GB), :].astype(BF16)
        y = (y + jnp.dot(xp, v_ref[g], preferred_element_type=F32)).astype(BF16)
        for half in range(halves):
            cat_ref[:, half * half_w + g * LANES:half * half_w + (g + 1) * LANES] = (
                y[:, half * LANES:(half + 1) * LANES])

    for half in range(halves):
        z = jnp.dot(cat_ref[:, half * half_w:(half + 1) * half_w], p_ref[...],
                    preferred_element_type=F32)
        for t in range(SSM_GB):
            y_ref[pl.ds(half * SSM_GB + t, n_chunks, stride=SSM_T), :] = (
                z[:, t * LANES:(t + 1) * LANES])


def _s5(u_slabs, mats, batch, seq, cast=()):
    wmat, mmat, vmat, dec_a, dec_b = mats
    m = u_slabs.shape[1]
    n_chunks = seq // SSM_T
    perm = _chunk_permutation()
    cast_specs = _side_cast_specs(cast, SSM_SLABS * batch, lambda k, b: k * batch + b)

    def grp(shape):
        return pl.BlockSpec((SSM_GB,) + shape, lambda k, b: (k,) + (0,) * len(shape))

    seq_spec = pl.BlockSpec((None, seq, LANES), lambda k, b: (k, b, 0))
    outs = pl.pallas_call(
        functools.partial(_s5_kernel, n_chunks=n_chunks, n_cast=len(cast)),
        grid=(SSM_SLABS, batch),
        in_specs=[
            seq_spec,
            pl.BlockSpec(perm.shape, lambda k, b: (0, 0)),
            grp((SSM_CW, 2 * SSM_SW)),
            grp((SSM_CW, SSM_CW)),
            grp((SSM_SW, SSM_CW)),
            grp((SSM_SW,)),
            grp((SSM_SW,)),
        ] + cast_specs,
        out_specs=[seq_spec] + cast_specs,
        out_shape=[jax.ShapeDtypeStruct((SSM_SLABS, m, LANES), F32)]
        + [jax.ShapeDtypeStruct(w.shape, BF16) for w in cast],
        scratch_shapes=[
            pltpu.VMEM((n_chunks, SSM_T * LANES), BF16),
            pltpu.VMEM((SSM_GB, n_chunks, SSM_CW), BF16),
            pltpu.VMEM((SSM_GB * n_chunks, SSM_SW), F32),
            pltpu.VMEM((SSM_GB * n_chunks, SSM_SW), F32),
            pltpu.VMEM((SSM_GB * n_chunks, SSM_SW), F32),
        ],
        compiler_params=_params(("parallel", "parallel")),
        name="s5_chunks",
    )(u_slabs, perm, wmat, mmat, vmat, dec_a, dec_b, *cast)
    return outs[0], outs[1:]


def _outproj_kernel(x_ref, yr_ref, ys_ref, wglu_ref, bglu_ref, og_ref, wout_ref, *rest, n_cast):
    o_ref = rest[n_cast]
    _side_cast(rest[:n_cast], rest[n_cast + 1:])
    half = D_MODEL // 2
    yr = yr_ref[...]
    acc_lo = jnp.dot(yr, wout_ref[:RET_WIDTH, :half], preferred_element_type=F32)
    y1 = jax.nn.gelu(jnp.concatenate([ys_ref[k] for k in range(SSM_SLABS)], axis=-1))
    z = jnp.dot(y1.astype(BF16), wglu_ref[...], preferred_element_type=F32) + bglu_ref[...]
    acc_hi = jnp.dot(yr, wout_ref[:RET_WIDTH, half:], preferred_element_type=F32)
    y2 = y1 * jax.nn.sigmoid(z)
    y_ssm = _rms(y2, og_ref[...]).astype(BF16)
    o_ref[:, :half] = x_ref[:, :half] + acc_lo + jnp.dot(
        y_ssm, wout_ref[RET_WIDTH:, :half], preferred_element_type=F32)
    o_ref[:, half:] = x_ref[:, half:] + acc_hi + jnp.dot(
        y_ssm, wout_ref[RET_WIDTH:, half:], preferred_element_type=F32)


def _outproj(x2d, y_ret, y_s5, w_glu, b_glu, out_g, w_out, cast=()):
    m = x2d.shape[0]
    cast_specs = _side_cast_specs(cast, m // ROW_TILE, lambda i: i)

    def rows(width):
        return pl.BlockSpec((ROW_TILE, width), lambda i: (i, 0))

    def whole(shape):
        return pl.BlockSpec(shape, lambda i: (0, 0))

    outs = pl.pallas_call(
        functools.partial(_outproj_kernel, n_cast=len(cast)),
        grid=(m // ROW_TILE,),
        in_specs=[
            rows(D_MODEL), rows(RET_WIDTH),
            pl.BlockSpec((SSM_SLABS, ROW_TILE, LANES), lambda i: (0, i, 0)),
            whole((SSM_WIDTH, SSM_WIDTH)), whole((1, SSM_WIDTH)), whole((1, SSM_WIDTH)),
            whole((D_MODEL, D_MODEL)),
        ] + cast_specs,
        out_specs=[rows(D_MODEL)] + cast_specs,
        out_shape=[jax.ShapeDtypeStruct((m, D_MODEL), F32)]
        + [jax.ShapeDtypeStruct(w.shape, BF16) for w in cast],
        compiler_params=_params(("parallel",)),
        name="outproj",
    )(x2d, y_ret, y_s5, w_glu, b_glu, out_g, w_out, *cast)
    return outs[0], outs[1:]


def _ffn_kernel(x_ref, g_ref, wg_ref, wu_ref, wd_ref, gf_ref, o_ref, h_ref):
    j = pl.program_id(1)
    last = pl.num_programs(1) - 1

    def step(first, final):
        if first:
            x = x_ref[...]
            h = _rms(x, g_ref[...]).astype(BF16)
            h_ref[...] = h
            base = x
        else:
            h = h_ref[...]
            base = o_ref[...]
        o = base
        acts = []
        for s in range(FF_SPLIT):
            cols = slice(s * (FF_TILE // FF_SPLIT), (s + 1) * (FF_TILE // FF_SPLIT))
            gate = jnp.dot(h, wg_ref[:, cols], preferred_element_type=F32)
            up = jnp.dot(h, wu_ref[:, cols], preferred_element_type=F32)
            acts.append((cols, (jax.nn.silu(gate) * up).astype(BF16)))
        for cols, act in acts:
            o = o + jnp.dot(act, wd_ref[cols, :], preferred_element_type=F32)
        o_ref[...] = _rms(o, gf_ref[...]) if final else o

    pl.when(j == 0)(lambda: step(True, False))
    pl.when((j > 0) & (j < last))(lambda: step(False, False))
    pl.when(j == last)(lambda: step(False, True))


def _ffn(x2d, gain, w_gate, w_up, w_down, gain_final):
    m = x2d.shape[0]
    return pl.pallas_call(
        _ffn_kernel,
        grid=(m // FFN_ROW_TILE, D_FF // FF_TILE),
        in_specs=[
            pl.BlockSpec((FFN_ROW_TILE, D_MODEL), lambda i, j: (i, 0)),
            pl.BlockSpec((1, D_MODEL), lambda i, j: (0, 0)),
            pl.BlockSpec((D_MODEL, FF_TILE), lambda i, j: (0, j)),
            pl.BlockSpec((D_MODEL, FF_TILE), lambda i, j: (0, j)),
            pl.BlockSpec((FF_TILE, D_MODEL), lambda i, j: (j, 0)),
            pl.BlockSpec((1, D_MODEL), lambda i, j: (0, 0)),
        ],
        out_specs=pl.BlockSpec((FFN_ROW_TILE, D_MODEL), lambda i, j: (i, 0)),
        out_shape=jax.ShapeDtypeStruct((m, D_MODEL), F32),
        scratch_shapes=[pltpu.VMEM((FFN_ROW_TILE, D_MODEL), BF16)],
        compiler_params=_params(("parallel", "arbitrary")),
        name="ffn",
    )(x2d, gain, w_gate, w_up, w_down, gain_final)


def _rope_tables(seq):
    half = RET_HEAD_DIM // 2
    pos = np.arange(seq, dtype=np.float64)
    freqs = ROPE_BASE ** (-np.arange(half, dtype=np.float64) / half)
    ang = pos[:, None] * freqs[None, :]
    cos = np.concatenate([np.cos(ang), np.cos(ang)], axis=-1)
    sin = np.concatenate([-np.sin(ang), np.sin(ang)], axis=-1)
    scale = np.array([1.0, RET_HEAD_DIM ** -0.5])[:, None, None]
    return (jnp.asarray(cos[None] * scale, dtype=F32), jnp.asarray(sin[None] * scale, dtype=F32))


def kernel(x, norm_mix_g, w_in, ret_gn_g, ssm_a_re, ssm_a_im, ssm_log_dt, ssm_b_re, ssm_b_im,
           ssm_c_re, ssm_c_im, ssm_d, ssm_w_glu, ssm_b_glu, ssm_out_g, w_out, norm_ffn_g,
           w_gate, w_up, w_down, norm_final_g):
    batch, seq, d = x.shape
    depth = w_in.shape[0]
    assert d == D_MODEL and seq % ROW_TILE == 0 and seq % RET_BLOCK == 0 and seq % SSM_T == 0
    m = batch * seq
    rope = _rope_tables(seq)
    x2d = x.reshape(m, d)

    for l in range(depth):
        mats, (w_in_b,) = _s5_prep(
            ssm_a_re[l], ssm_a_im[l], ssm_log_dt[l], ssm_b_re[l], ssm_b_im[l], ssm_c_re[l],
            ssm_c_im[l], ssm_d[l], cast=(w_in[l],))
        (u_slabs, y_ret), (w_gate_b,) = _mix_in(x2d, norm_mix_g[l][None], w_in_b, rope,
                                                ret_gn_g[l][None], seq, cast=(w_gate[l],))
        y_s5, (w_down_b, w_out_b, w_glu_b) = _s5(u_slabs, mats, batch, seq,
                                                 cast=(w_down[l], w_out[l], ssm_w_glu[l]))

        x2d, (w_up_b,) = _outproj(x2d, y_ret, y_s5, w_glu_b, ssm_b_glu[l][None],
                                  ssm_out_g[l][None], w_out_b, cast=(w_up[l],))
        last = l == depth - 1
        assert last, "fused final norm assumes a single layer"
        x2d = _ffn(x2d, norm_ffn_g[l][None], w_gate_b, w_up_b, w_down_b, norm_final_g[None])
    return x2d.reshape(batch, seq, d)
```
